```python
import jax, jax.numpy as jnp
from jax import lax
import numpy as np

D_MODEL = 1024
BATCH = 32
SEQ = 256
DEPTH = 1
DEC_BATCH = 2
DEC_SEQ = 1024
PAST_LEN = 512

GRID_W = 64
D_A = 512
N_HEADS_A = 4
HEAD_A = D_A // N_HEADS_A
N_DIRS = 2
D_B = 512
N_POOL_GROUPS = 4
POOL_GROUP = D_B // N_POOL_GROUPS
POOL_WINDOWS = (2, 4, 8, 16)
D_FF = -(-8 * D_MODEL // (3 * 256)) * 256
CHUNK = 32
EPS = 1e-6
SPLITS = [D_A, 2 * D_A, 3 * D_A, 4 * D_A, 5 * D_A, 5 * D_A + D_B, 5 * D_A + D_B + D_MODEL]
IN_COLS = 5 * D_A + D_B + 2 * D_MODEL

kernel_name = 'hybrid_hgrn2_pool_prefix_dit_step'


def rmsnorm(x, g):
    x32 = x.astype(jnp.float32)
    y = x32 * lax.rsqrt(jnp.mean(x32 * x32, axis=-1, keepdims=True) + EPS)
    return (y * g.astype(jnp.float32)).astype(x.dtype)


def hgrn2_scan(q, k, v, logf, s0):
    B, T, H, _ = q.shape
    n = T // CHUNK

    def to_chunks(a):
        return a.astype(jnp.float32).reshape(B, n, CHUNK, H, a.shape[-1]).transpose(1, 0, 3, 2, 4)

    qc, kc, vc, gc = to_chunks(q), to_chunks(k), to_chunks(v), to_chunks(logf)
    lower = jnp.tril(jnp.ones((CHUNK, CHUNK), dtype=bool))[:, :, None]

    def step(S, inp):
        qt, kt, vt, gt = inp
        b = jnp.cumsum(gt, axis=-2)
        o_inter = jnp.einsum('bhtd,bhde->bhte', qt * jnp.exp(b), S)
        diff = b[:, :, :, None, :] - b[:, :, None, :, :]
        decay = jnp.exp(jnp.where(lower, diff, -jnp.inf))
        scores = jnp.einsum('bhtd,bhsd,bhtsd->bhts', qt, kt, decay)
        o_intra = jnp.einsum('bhts,bhse->bhte', scores, vt)
        b_last = b[:, :, -1, :]
        S_new = jnp.exp(b_last)[..., None] * S + jnp.einsum(
            'bhsd,bhse->bhde', kt * jnp.exp(b_last[:, :, None, :] - b), vt)
        return S_new, o_inter + o_intra

    S_fin, o = lax.scan(step, s0.astype(jnp.float32), (qc, kc, vc, gc))
    o = o.transpose(1, 0, 3, 2, 4).reshape(B, T, H, -1)
    return o, S_fin


def hgrn2_branch(zq, zf_fwd, zf_bwd, zi, zog, lb, norm_g, s0_fwd, s0_bwd):
    B, T, _ = zq.shape
    heads = lambda a: a.reshape(B, T, N_HEADS_A, HEAD_A)

    def gate(zf, lbd):
        f = lbd + (1.0 - lbd) * jax.nn.sigmoid(zf.astype(jnp.float32))
        return heads(jnp.log(f)), heads(1.0 - f)

    q, v = heads(zq), heads(zi)
    logf_f, k_f = gate(zf_fwd, lb[0])
    logf_b, k_b = gate(zf_bwd, lb[1])
    o_f, S_f = hgrn2_scan(q, k_f, v, logf_f, s0_fwd)
    flip = lambda a: jnp.flip(a, axis=1)
    o_b, S_b = hgrn2_scan(flip(q), flip(k_b), flip(v), flip(logf_b), s0_bwd)
    o = o_f + flip(o_b)
    o = o * lax.rsqrt(jnp.mean(o * o, axis=-1, keepdims=True) + EPS)
    o = o * norm_g.astype(jnp.float32).reshape(N_HEADS_A, HEAD_A)
    o = o.reshape(B, T, D_A) * jax.nn.silu(zog.astype(jnp.float32))
    return o.astype(zq.dtype), S_f, S_b


def pool_minus_self_1d(x):
    B, T, _ = x.shape
    x32 = x.astype(jnp.float32)
    cs = jnp.concatenate([jnp.zeros((B, 1, D_B), jnp.float32), jnp.cumsum(x32, axis=1)], axis=1)
    pos = jnp.arange(T)
    outs = []
    for g, w in enumerate(POOL_WINDOWS):
        lo = jnp.clip(pos - w // 2, 0, T - 1)
        hi = jnp.clip(pos + w // 2 - 1, 0, T - 1)
        csg = cs[..., g * POOL_GROUP:(g + 1) * POOL_GROUP]
        s = jnp.take(csg, hi + 1, axis=1) - jnp.take(csg, lo, axis=1)
        cnt = (hi - lo + 1).astype(jnp.float32)[None, :, None]
        outs.append(s / cnt)
    return (jnp.concatenate(outs, axis=-1) - x32).astype(x.dtype)


def pool_minus_self_2d(x):
    B, T, _ = x.shape
    rows = T // GRID_W
    xg = x.astype(jnp.float32).reshape(B, rows, GRID_W, D_B)
    sat = jnp.pad(jnp.cumsum(jnp.cumsum(xg, axis=1), axis=2), ((0, 0), (1, 0), (1, 0), (0, 0)))
    r, cidx = jnp.arange(rows), jnp.arange(GRID_W)
    outs = []
    for g, w in enumerate(POOL_WINDOWS):
        rlo, rhi = jnp.clip(r - w // 2, 0, rows - 1), jnp.clip(r + w // 2 - 1, 0, rows - 1)
        clo, chi = jnp.clip(cidx - w // 2, 0, GRID_W - 1), jnp.clip(cidx + w // 2 - 1, 0, GRID_W - 1)
        sg = sat[..., g * POOL_GROUP:(g + 1) * POOL_GROUP]
        rect = lambda ri, ci: jnp.take(jnp.take(sg, ri, axis=1), ci, axis=2)
        s = rect(rhi + 1, chi + 1) - rect(rlo, chi + 1) - rect(rhi + 1, clo) + rect(rlo, clo)
        cnt = ((rhi - rlo + 1)[:, None] * (chi - clo + 1)[None, :]).astype(jnp.float32)
        outs.append(s / cnt[None, :, :, None])
    pooled = jnp.concatenate(outs, axis=-1)
    return (pooled - xg).reshape(B, T, D_B).astype(x.dtype)


def trunk_layer(x, mod, s0_fwd, s0_bwd, grid, lb, norm1_g, w_in, hgrn_norm_g, w_branch_a,
                pool_w, pool_scale, w_branch_b, w_out, norm2_g, w_ffn_in, w_ffn_out):
    B, T, _ = x.shape
    shift1, scale1, gate1, shift2, scale2, gate2 = [m[:, None, :] for m in jnp.split(mod, 6, axis=-1)]
    h = rmsnorm(x, norm1_g) * (1.0 + scale1) + shift1
    zq, zf_fwd, zf_bwd, zi, zog, zpool, zga, zgb = jnp.split(h @ w_in, SPLITS, axis=-1)
    o_a, S_f, S_b = hgrn2_branch(zq, zf_fwd, zf_bwd, zi, zog, lb, hgrn_norm_g, s0_fwd, s0_bwd)
    pm = pool_minus_self_2d(zpool) if grid else pool_minus_self_1d(zpool)
    o_b = jnp.einsum('btgc,gcd->btgd', pm.reshape(B, T, N_POOL_GROUPS, POOL_GROUP), pool_w)
    o_b = o_b.reshape(B, T, D_B) * pool_scale
    merged = jax.nn.sigmoid(zga) * (o_a @ w_branch_a) + jax.nn.sigmoid(zgb) * (o_b @ w_branch_b)
    x = x + gate1 * (merged @ w_out)
    h2 = rmsnorm(x, norm2_g) * (1.0 + scale2) + shift2
    g_part, u_part = jnp.split(h2 @ w_ffn_in, 2, axis=-1)
    x = x + gate2 * ((jax.nn.silu(g_part) * u_part) @ w_ffn_out)
    return x, S_f, S_b


def setup_inputs(seed: int = 0) -> dict:
    key = jax.random.key(seed)
    ks = jax.random.split(key, 20)
    nrm = lambda k, shape, scale: jax.random.normal(k, shape, jnp.float32) * scale
    return {
        'x_prompt': nrm(ks[0], (BATCH, SEQ, D_MODEL), 1.0),
        'x_sample': nrm(ks[1], (DEC_BATCH, DEC_SEQ, D_MODEL), 1.0),
        'state_hgrn': nrm(ks[2], (DEC_BATCH, DEPTH, N_DIRS, N_HEADS_A, HEAD_A, HEAD_A), 0.5),
        'c': nrm(ks[3], (DEC_BATCH, D_MODEL), 1.0),
        'c_ctx': nrm(ks[4], (D_MODEL,), 1.0),
        'w_ada': nrm(ks[5], (DEPTH, D_MODEL, 6 * D_MODEL), D_MODEL ** -0.5),
        'b_ada': nrm(ks[6], (DEPTH, 6 * D_MODEL), 0.02),
        'norm1_g': 1.0 + nrm(ks[7], (DEPTH, D_MODEL), 0.05),
        'w_in': nrm(ks[8], (DEPTH, D_MODEL, IN_COLS), D_MODEL ** -0.5),
        'hgrn_lb_logits': nrm(ks[9], (DEPTH + 1, N_DIRS, D_A), 0.5),
        'hgrn_norm_g': 1.0 + nrm(ks[10], (DEPTH, D_A), 0.05),
        'w_branch_a': nrm(ks[11], (DEPTH, D_A, D_MODEL), D_A ** -0.5),
        'pool_w': nrm(ks[12], (DEPTH, N_POOL_GROUPS, POOL_GROUP, POOL_GROUP), POOL_GROUP ** -0.5),
        'pool_scale': 1.0 + nrm(ks[13], (DEPTH, D_B), 0.05),
        'w_branch_b': nrm(ks[14], (DEPTH, D_B, D_MODEL), D_B ** -0.5),
        'w_out': nrm(ks[15], (DEPTH, D_MODEL, D_MODEL), D_MODEL ** -0.5),
        'norm2_g': 1.0 + nrm(ks[16], (DEPTH, D_MODEL), 0.05),
        'w_ffn_in': nrm(ks[17], (DEPTH, D_MODEL, 2 * D_FF), D_MODEL ** -0.5),
        'w_ffn_out': nrm(ks[18], (DEPTH, D_FF, D_MODEL), D_FF ** -0.5),
        'final_g': 1.0 + nrm(ks[19], (D_MODEL,), 0.05),
    }


def reference(x_prompt, x_sample, state_hgrn, c, c_ctx, w_ada, b_ada, norm1_g, w_in,
              hgrn_lb_logits, hgrn_norm_g, w_branch_a, pool_w, pool_scale, w_branch_b,
              w_out, norm2_g, w_ffn_in, w_ffn_out, final_g):
    lb_all = jnp.cumsum(jax.nn.softmax(hgrn_lb_logits.astype(jnp.float32), axis=0), axis=0)
    B_p = x_prompt.shape[0]
    zeros_state = jnp.zeros((B_p, N_HEADS_A, HEAD_A, HEAD_A), jnp.float32)
    xp, xs = x_prompt, x_sample
    ctx_states = []
    for l in range(DEPTH):
        layer_w = (lb_all[l], norm1_g[l], w_in[l], hgrn_norm_g[l], w_branch_a[l], pool_w[l],
                   pool_scale[l], w_branch_b[l], w_out[l], norm2_g[l], w_ffn_in[l], w_ffn_out[l])
        mod_ctx = jax.nn.silu(c_ctx[None, :]) @ w_ada[l] + b_ada[l]
        mod_lat = jax.nn.silu(c) @ w_ada[l] + b_ada[l]
        xp, S_f, S_b = trunk_layer(xp, mod_ctx, zeros_state, zeros_state, False, *layer_w)
        ctx_states.append(jnp.stack([S_f, S_b], axis=1))
        xs, _, _ = trunk_layer(xs, mod_lat, state_hgrn[:, l, 0], state_hgrn[:, l, 1], True, *layer_w)
    y_prompt = rmsnorm(xp, final_g)
    y_sample = rmsnorm(xs, final_g)
    new_state_hgrn = jnp.stack(ctx_states, axis=1).astype(x_prompt.dtype)
    return (y_prompt, y_sample, new_state_hgrn)
```

```python
import functools

import jax
import jax.numpy as jnp
from jax import lax
from jax.experimental import pallas as pl
from jax.experimental.pallas import tpu as pltpu

F32 = jnp.float32
BF16 = jnp.bfloat16

D_MODEL = 1024
D_A = 512
N_HEADS = 4
HEAD = D_A // N_HEADS
D_B = 512
N_GROUPS = 4
GROUP = D_B // N_GROUPS
POOL_WINDOWS = (2, 4, 8, 16)
GRID_W = 64
D_FF = 2816
IN_COLS = 5 * D_A + D_B + 2 * D_MODEL
EPS = 1e-6

CHUNK = 128
DIAG = 16
ROWS = 256
EXP_CLAMP = 80.0
VMEM_LIMIT_V7X = 58 * 1024 * 1024


def _sigmoid(x):
  return 1.0 / (1.0 + jnp.exp(-x))


def _dot(a, b):
  return jnp.dot(a, b, preferred_element_type=F32)


def _dot_nt(a, b):
  return lax.dot_general(a, b, (((1,), (1,)), ((), ())), preferred_element_type=F32)


def _dot_tn(a, b):
  return lax.dot_general(a, b, (((0,), (0,)), ((), ())), preferred_element_type=F32)


def _split_bf16(x):
  hi = x.astype(BF16)
  lo = (x - hi.astype(F32)).astype(BF16)
  return hi, lo


def _rms_rows(x):
  return x * lax.rsqrt(jnp.mean(x * x, axis=-1, keepdims=True) + EPS)


def _mod_body(c_ref, w_ref, b_ref, o_ref):
  cv = c_ref[...]
  s = (cv * _sigmoid(cv)).astype(BF16)
  o_ref[...] = _dot(s, w_ref[...].astype(BF16)) + b_ref[...]


def _mod_call(cvec, w_ada, b_ada):
  n = w_ada.shape[1]
  bn = 1536
  return pl.pallas_call(
      _mod_body,
      grid=(n // bn,),
      in_specs=[
          pl.BlockSpec((8, D_MODEL), lambda j: (0, 0)),
          pl.BlockSpec((D_MODEL, bn), lambda j: (0, j)),
          pl.BlockSpec((1, bn), lambda j: (0, j)),
      ],
      out_specs=pl.BlockSpec((8, bn), lambda j: (0, j)),
      out_shape=jax.ShapeDtypeStruct((8, n), F32),
      compiler_params=pltpu.CompilerParams(
          dimension_semantics=("arbitrary",), vmem_limit_bytes=VMEM_LIMIT_V7X),
      name="adaln_mod",
  )(cvec, w_ada, b_ada)


def _level_ref_rows(level, dirn):
  if level == 0:
    return DIAG, (DIAG // 2 if dirn == 0 else DIAG // 2 - 1)
  half = DIAG << (level - 1)
  return 2 * half, (half - 1 if dirn == 0 else half)


N_LEVELS = 1 + (CHUNK // DIAG).bit_length() - 1


def _mix_body(*refs, layer, n_seq, seq_len, grid2d, has_state, emit_state):
  (x_ref, mod_ref, lbl_ref, n1_ref, win_ref, ng_ref, wa_ref, pw_ref, ps_ref, wb_ref,
   wo_ref) = refs[:11]
  rest = list(refs[11:])
  s0_ref = rest.pop(0) if has_state else None
  out_ref = rest.pop(0)
  st_ref = rest.pop(0) if emit_state else None
  (h_s, q_s, zf_s, v_s, og_s, zp_s, ga_s, gb_s, b_c, k_c, o_s, pm_s, st_s, tri_s, lev_s,
   band_s, icnt_s) = rest[:17]
  cp_s = rest[17] if grid2d else None

  tt = n_seq * seq_len
  cps = seq_len // CHUNK

  ti = lax.broadcasted_iota(jnp.int32, (CHUNK, CHUNK), 0)
  si = lax.broadcasted_iota(jnp.int32, (CHUNK, CHUNK), 1)
  xr = ti ^ si
  lvl = jnp.zeros((CHUNK, CHUNK), jnp.int32)
  for j in range(1, N_LEVELS):
    lvl = jnp.where(xr >= (DIAG << (j - 1)), j, lvl)
  lev_s[0] = jnp.where(ti >= si, lvl, -1)
  lev_s[1] = jnp.where(ti <= si, lvl, -1)
  tri_s[0] = (ti >= si).astype(BF16)
  tri_s[1] = (ti <= si).astype(BF16)

  seg_shift = (GRID_W if grid2d else ROWS).bit_length() - 1
  tr = lax.broadcasted_iota(jnp.int32, (ROWS, ROWS), 0)
  sr = lax.broadcasted_iota(jnp.int32, (ROWS, ROWS), 1)
  same_seg = (tr >> seg_shift) == (sr >> seg_shift)
  for g, w in enumerate(POOL_WINDOWS):
    dlt = sr - tr
    band = (same_seg & (dlt >= -(w // 2)) & (dlt <= w // 2 - 1)).astype(F32)
    band_s[g] = band.astype(BF16)
    icnt_s[g] = 1.0 / jnp.sum(band, axis=1, keepdims=True)

  mod = mod_ref[...]
  sh1 = mod[:, 0:D_MODEL]
  a1 = n1_ref[...] * (1.0 + mod[:, D_MODEL:2 * D_MODEL])
  gate1 = mod[:, 2 * D_MODEL:3 * D_MODEL]
  lg = lbl_ref[...]
  e = jnp.exp(lg - jnp.max(lg, axis=0, keepdims=True))
  lb = jnp.sum(e[:layer + 1], axis=0) / jnp.sum(e, axis=0)

  for i in range(tt // ROWS):
    r = slice(i * ROWS, (i + 1) * ROWS)
    h_s[r, :] = (_rms_rows(x_ref[r, :]) * a1 + sh1).astype(BF16)
  hfull = h_s[...]

  def proj(c0, c1):
    return _dot(hfull, win_ref[:, c0:c1])

  q_s[...] = proj(0, D_A)
  zf_s[:, 0:D_A] = proj(D_A, 2 * D_A)
  zf_s[:, D_A:2 * D_A] = proj(2 * D_A, 3 * D_A)
  v_s[...] = proj(3 * D_A, 4 * D_A).astype(BF16)
  zo = proj(4 * D_A, 5 * D_A)
  og_s[...] = (zo * _sigmoid(zo)).astype(BF16)
  zp_s[...] = proj(5 * D_A, 5 * D_A + D_B)
  c0 = 5 * D_A + D_B
  for j in range(D_MODEL // 512):
    ga_s[:, j * 512:(j + 1) * 512] = _sigmoid(proj(c0 + j * 512, c0 + (j + 1) * 512)).astype(BF16)
  c0 += D_MODEL
  for j in range(D_MODEL // 512):
    gb_s[:, j * 512:(j + 1) * 512] = _sigmoid(proj(c0 + j * 512, c0 + (j + 1) * 512)).astype(BF16)

  def chunk(base, dirn):
    rows = pl.ds(base, CHUNK)
    lbd = lb[dirn:dirn + 1, :]
    f = lbd + (1.0 - lbd) * _sigmoid(zf_s[rows, dirn * D_A:(dirn + 1) * D_A])
    ghi, glo = _split_bf16(jnp.log(f))
    tri = tri_s[dirn]
    b_c[...] = _dot(tri, ghi) + _dot(tri, glo)
    k_c[...] = 1.0 - f
    for hd in range(N_HEADS):
      cs = slice(hd * HEAD, (hd + 1) * HEAD)
      bh = b_c[:, cs]
      qh = q_s[rows, cs]
      kh = k_c[:, cs]
      vh = v_s[rows, cs]
      lev = lev_s[dirn]
      sc = jnp.zeros((CHUNK, CHUNK), F32)
      for level in range(N_LEVELS):
        blk, rrow = _level_ref_rows(level, dirn)
        pieces = [jnp.broadcast_to(b_c[p * blk + rrow:p * blk + rrow + 1, cs], (blk, HEAD))
                  for p in range(CHUNK // blk)]
        ref = pieces[0] if len(pieces) == 1 else jnp.concatenate(pieces, axis=0)
        clamp = EXP_CLAMP if level == 0 else 0.0
        fa = (qh * jnp.exp(jnp.minimum(bh - ref, clamp))).astype(BF16)
        fb = (kh * jnp.exp(jnp.minimum(ref - bh, clamp))).astype(BF16)
        sc = jnp.where(lev == level, _dot_nt(fa, fb), sc)
      tot_row = CHUNK - 1 if dirn == 0 else 0
      tot = b_c[tot_row:tot_row + 1, cs]
      qt = (qh * jnp.exp(bh)).astype(BF16)
      kt = (kh * jnp.exp(tot - bh)).astype(BF16)
      st = st_s[hd]
      o = _dot(sc.astype(BF16), vh) + _dot_nt(qt, st.astype(BF16))
      st_s[hd] = st * jnp.exp(tot) + _dot_tn(vh, kt)
      if dirn == 0:
        o_s[rows, cs] = o
      else:
        o_s[rows, cs] += o

  for dirn in range(2):
    for sq in range(n_seq):
      for hd in range(N_HEADS):
        if has_state:
          st_s[hd] = s0_ref[dirn, hd].T
        else:
          st_s[hd] = jnp.zeros((HEAD, HEAD), F32)

      def body(ci, carry, dirn=dirn, sq=sq):
        c = ci if dirn == 0 else cps - 1 - ci
        chunk(pl.multiple_of(sq * seq_len + c * CHUNK, CHUNK), dirn)
        return carry

      lax.fori_loop(0, cps, body, 0)
      if emit_state:
        for hd in range(N_HEADS):
          st_ref[sq, dirn, hd] = st_s[hd].T

  def band_pool(r, g):
    gs = slice(g * GROUP, (g + 1) * GROUP)
    zhi, zlo = _split_bf16(zp_s[r, gs])
    return (_dot(band_s[g], zhi) + _dot(band_s[g], zlo)) * icnt_s[g]

  if not grid2d:
    for i in range(tt // ROWS):
      r = slice(i * ROWS, (i + 1) * ROWS)
      for g in range(N_GROUPS):
        gs = slice(g * GROUP, (g + 1) * GROUP)
        pm_s[r, gs] = (band_pool(r, g) - zp_s[r, gs]).astype(BF16)
  else:
    n_rows = seq_len // GRID_W
    for i in range(tt // ROWS):
      r = slice(i * ROWS, (i + 1) * ROWS)
      for g in range(N_GROUPS):
        cp_s[r, g * GROUP:(g + 1) * GROUP] = band_pool(r, g)
    for g, w in enumerate(POOL_WINDOWS):
      gs = slice(g * GROUP, (g + 1) * GROUP)
      acc = None
      lo_prev = hi_prev = None
      for rr in range(n_rows):
        lo = max(rr - w // 2, 0)
        hi = min(rr + w // 2 - 1, n_rows - 1)
        if rr == 0:
          acc = cp_s[0:GRID_W, gs]
          for k in range(1, hi + 1):
            acc = acc + cp_s[k * GRID_W:(k + 1) * GRID_W, gs]
        else:
          if hi > hi_prev:
            acc = acc + cp_s[hi * GRID_W:(hi + 1) * GRID_W, gs]
          if lo > lo_prev:
            acc = acc - cp_s[lo_prev * GRID_W:(lo_prev + 1) * GRID_W, gs]
        lo_prev, hi_prev = lo, hi
        rs = slice(rr * GRID_W, (rr + 1) * GRID_W)
        pm_s[rs, gs] = (acc * (1.0 / (hi - lo + 1)) - zp_s[rs, gs]).astype(BF16)

  ng = ng_ref[...]
  psc = ps_ref[...]
  for i in range(tt // ROWS):
    r = slice(i * ROWS, (i + 1) * ROWS)
    parts = []
    for hd in range(N_HEADS):
      cs = slice(hd * HEAD, (hd + 1) * HEAD)
      parts.append(_rms_rows(o_s[r, cs]) * ng[:, cs])
    oa = (jnp.concatenate(parts, axis=1) * og_s[r, :].astype(F32)).astype(BF16)
    ya = _dot(oa, wa_ref[...])
    obp = []
    for g in range(N_GROUPS):
      gs = slice(g * GROUP, (g + 1) * GROUP)
      obp.append(_dot(pm_s[r, gs], pw_ref[g]) * psc[:, gs])
    ob = jnp.concatenate(obp, axis=1).astype(BF16)
    yb = _dot(ob, wb_ref[...])
    merged = (ga_s[r, :].astype(F32) * ya + gb_s[r, :].astype(F32) * yb).astype(BF16)
    out_ref[r, :] = x_ref[r, :] + gate1 * _dot(merged, wo_ref[...])


def _const_spec(shape):
  nd = len(shape)
  return pl.BlockSpec(shape, lambda i, _nd=nd: (0,) * _nd, pipeline_mode=pl.Buffered(1))


def _mix_call(x2d, mod3, mod_row, lbl, n1, win, ng, wa, pw, ps, wb, wo, s0, *, layer, n_seq,
              seq_len, grid2d, emit_state):
  tokens = x2d.shape[0]
  tt = n_seq * seq_len
  assert tokens % tt == 0 and seq_len % CHUNK == 0 and tt % ROWS == 0
  steps = tokens // tt
  has_state = s0 is not None
  in_specs = [
      pl.BlockSpec((tt, D_MODEL), lambda i: (i, 0)),
      pl.BlockSpec((None, 1, 6 * D_MODEL), lambda i: (mod_row(i), 0, 0)),
      _const_spec(lbl.shape), _const_spec(n1.shape), _const_spec(win.shape), _const_spec(ng.shape),
      _const_spec(wa.shape), _const_spec(pw.shape), _const_spec(ps.shape), _const_spec(wb.shape),
      _const_spec(wo.shape),
  ]
  args = [x2d, mod3, lbl, n1, win, ng, wa, pw, ps, wb, wo]
  if has_state:
    in_specs.append(pl.BlockSpec((None, None, 2, N_HEADS, HEAD, HEAD),
                                 lambda i: (i, layer, 0, 0, 0, 0)))
    args.append(s0)
  out_specs = [pl.BlockSpec((tt, D_MODEL), lambda i: (i, 0))]
  out_shape = [jax.ShapeDtypeStruct((tokens, D_MODEL), F32)]
  if emit_state:
    out_specs.append(pl.BlockSpec((n_seq, 2, N_HEADS, HEAD, HEAD), lambda i: (i, 0, 0, 0, 0)))
    out_shape.append(jax.ShapeDtypeStruct((tokens // seq_len, 2, N_HEADS, HEAD, HEAD), F32))
  scratch = [
      pltpu.VMEM((tt, D_MODEL), BF16),
      pltpu.VMEM((tt, D_A), F32),
      pltpu.VMEM((tt, 2 * D_A), F32),
      pltpu.VMEM((tt, D_A), BF16),
      pltpu.VMEM((tt, D_A), BF16),
      pltpu.VMEM((tt, D_B), F32),
      pltpu.VMEM((tt, D_MODEL), BF16),
      pltpu.VMEM((tt, D_MODEL), BF16),
      pltpu.VMEM((CHUNK, D_A), F32),
      pltpu.VMEM((CHUNK, D_A), F32),
      pltpu.VMEM((tt, D_A), F32),
      pltpu.VMEM((tt, D_B), BF16),
      pltpu.VMEM((N_HEADS, HEAD, HEAD), F32),
      pltpu.VMEM((2, CHUNK, CHUNK), BF16),
      pltpu.VMEM((2, CHUNK, CHUNK), jnp.int32),
      pltpu.VMEM((N_GROUPS, ROWS, ROWS), BF16),
      pltpu.VMEM((N_GROUPS, ROWS, 1), F32),
  ]
  if grid2d:
    scratch.append(pltpu.VMEM((tt, D_B), F32))
  body = functools.partial(_mix_body, layer=layer, n_seq=n_seq, seq_len=seq_len, grid2d=grid2d,
                           has_state=has_state, emit_state=emit_state)
  return pl.pallas_call(
      body,
      grid=(steps,),
      in_specs=in_specs,
      out_specs=out_specs,
      out_shape=out_shape,
      scratch_shapes=scratch,
      compiler_params=pltpu.CompilerParams(
          dimension_semantics=("arbitrary",), vmem_limit_bytes=VMEM_LIMIT_V7X),
      name="mix_grid2d" if grid2d else "mix_seq1d",
  )(*args)


FFN_COLS = 256


def _ffn_body(x_ref, mod_ref, n2_ref, wi_ref, wo_ref, fg_ref, o_ref, h_s, hid_s):
  tt = x_ref.shape[0]
  mod = mod_ref[...]
  sh2 = mod[:, 3 * D_MODEL:4 * D_MODEL]
  a2 = n2_ref[...] * (1.0 + mod[:, 4 * D_MODEL:5 * D_MODEL])
  gate2 = mod[:, 5 * D_MODEL:6 * D_MODEL]
  for i in range(tt // ROWS):
    r = slice(i * ROWS, (i + 1) * ROWS)
    h_s[r, :] = (_rms_rows(x_ref[r, :]) * a2 + sh2).astype(BF16)
  h = h_s[...]
  for j in range(D_FF // FFN_COLS):
    cg = slice(j * FFN_COLS, (j + 1) * FFN_COLS)
    cu = slice(D_FF + j * FFN_COLS, D_FF + (j + 1) * FFN_COLS)
    gp = _dot(h, wi_ref[:, cg])
    up = _dot(h, wi_ref[:, cu])
    hid_s[:, cg] = (gp * _sigmoid(gp) * up).astype(BF16)
  o_ref[...] = _dot(hid_s[...], wo_ref[...])
  fg = fg_ref[...]
  for i in range(tt // ROWS):
    r = slice(i * ROWS, (i + 1) * ROWS)
    o_ref[r, :] = _rms_rows(x_ref[r, :] + gate2 * o_ref[r, :]) * fg


def _ffn_call(x2d, mod3, mod_row, n2, wi, wo, fg, *, tt):
  tokens = x2d.shape[0]
  assert tokens % tt == 0 and tt % ROWS == 0 and D_FF % FFN_COLS == 0
  return pl.pallas_call(
      _ffn_body,
      grid=(tokens // tt,),
      in_specs=[
          pl.BlockSpec((tt, D_MODEL), lambda i: (i, 0)),
          pl.BlockSpec((None, 1, 6 * D_MODEL), lambda i: (mod_row(i), 0, 0)),
          _const_spec(n2.shape), _const_spec(wi.shape), _const_spec(wo.shape),
          _const_spec(fg.shape),
      ],
      out_specs=pl.BlockSpec((tt, D_MODEL), lambda i: (i, 0)),
      out_shape=jax.ShapeDtypeStruct((tokens, D_MODEL), F32),
      scratch_shapes=[pltpu.VMEM((tt, D_MODEL), BF16), pltpu.VMEM((tt, D_FF), BF16)],
      compiler_params=pltpu.CompilerParams(
          dimension_semantics=("arbitrary",), vmem_limit_bytes=VMEM_LIMIT_V7X),
      name="ffn_final",
  )(x2d, mod3, n2, wi, wo, fg)


def kernel(x_prompt, x_sample, state_hgrn, c, c_ctx, w_ada, b_ada, norm1_g, w_in, hgrn_lb_logits,
           hgrn_norm_g, w_branch_a, pool_w, pool_scale, w_branch_b, w_out, norm2_g, w_ffn_in,
           w_ffn_out, final_g):
  bp, sp, _ = x_prompt.shape
  bs, ss, _ = x_sample.shape
  depth = w_in.shape[0]
  assert bs + 1 <= 8 and ss % GRID_W == 0
  xp = x_prompt.reshape(bp * sp, D_MODEL)
  xs = x_sample.reshape(bs * ss, D_MODEL)
  cvec = jnp.concatenate([c_ctx[None, :], c, jnp.zeros((8 - 1 - bs, D_MODEL), F32)], axis=0)
  row2 = lambda a: a.reshape(1, -1)
  ffn_tt = 512
  ctx_states = []
  for l in range(depth):
    mod3 = _mod_call(cvec, w_ada[l], row2(b_ada[l]))[:1 + bs, None, :]
    mix_w = (hgrn_lb_logits, row2(norm1_g[l]), w_in[l].astype(BF16), row2(hgrn_norm_g[l]),
             w_branch_a[l].astype(BF16), pool_w[l].astype(BF16), row2(pool_scale[l]),
             w_branch_b[l].astype(BF16), w_out[l].astype(BF16))
    ffn_w = (row2(norm2_g[l]), w_ffn_in[l].astype(BF16), w_ffn_out[l].astype(BF16))
    last = l == depth - 1
    fg = row2(final_g) if last else jnp.ones((1, D_MODEL), F32)
    xp, st = _mix_call(xp, mod3, lambda i: 0, *mix_w, None, layer=l, n_seq=2, seq_len=sp,
                       grid2d=False, emit_state=True)
    ctx_states.append(st)
    (xs,) = _mix_call(xs, mod3, lambda i: 1 + i, *mix_w, state_hgrn, layer=l, n_seq=1,
                      seq_len=ss, grid2d=True, emit_state=False)
    assert last, "the FFN call also applies the final norm"
    xp = _ffn_call(xp, mod3, lambda i: 0, *ffn_w, fg, tt=ffn_tt)
    xs = _ffn_call(xs, mod3, lambda i: 1 + i // (ss // ffn_tt), *ffn_w, fg, tt=ffn_tt)
  new_state = jnp.stack(ctx_states, axis=1).astype(x_prompt.dtype)
  return (xp.reshape(bp, sp, D_MODEL), xs.reshape(bs, ss, D_MODEL), new_state)
```

```python
import functools

import jax
import jax.numpy as jnp
from jax import lax
from jax.experimental import pallas as pl
from jax.experimental.pallas import tpu as pltpu

F32 = jnp.float32
BF16 = jnp.bfloat16

D_MODEL = 1024
D_A = 512
N_HEADS = 4
HEAD = D_A // N_HEADS
D_B = 512
N_GROUPS = 4
GROUP = D_B // N_GROUPS
POOL_WINDOWS = (2, 4, 8, 16)
GRID_W = 64
D_FF = 2816
IN_COLS = 5 * D_A + D_B + 2 * D_MODEL
EPS = 1e-6

CHUNK = 128
DIAG = 16
ROWS = 256
EXP_CLAMP = 80.0
VMEM_LIMIT_V7X = 58 * 1024 * 1024


def _sigmoid(x):
  return 0.5 + 0.5 * jnp.tanh(0.5 * x)


def _silu(x):
  h = 0.5 * x
  return h + h * jnp.tanh(h)


def _dot(a, b):
  return jnp.dot(a, b, preferred_element_type=F32)


def _dot_nt(a, b):
  return lax.dot_general(a, b, (((1,), (1,)), ((), ())), preferred_element_type=F32)


def _dot_tn(a, b):
  return lax.dot_general(a, b, (((0,), (0,)), ((), ())), preferred_element_type=F32)


def _split_bf16(x):
  hi = x.astype(BF16)
  lo = (x - hi.astype(F32)).astype(BF16)
  return hi, lo


def _rms_rows(x):
  return x * lax.rsqrt(jnp.mean(x * x, axis=-1, keepdims=True) + EPS)


def _mod_body(c_ref, w_ref, b_ref, o_ref):
  cv = c_ref[...]
  s = _silu(cv).astype(BF16)
  o_ref[...] = _dot(s, w_ref[...].astype(BF16)) + b_ref[...]


def _mod_call(cvec, w_ada, b_ada):
  n = w_ada.shape[1]
  bn = 1536
  return pl.pallas_call(
      _mod_body,
      grid=(n // bn,),
      in_specs=[
          pl.BlockSpec((8, D_MODEL), lambda j: (0, 0)),
          pl.BlockSpec((D_MODEL, bn), lambda j: (0, j)),
          pl.BlockSpec((1, bn), lambda j: (0, j)),
      ],
      out_specs=pl.BlockSpec((8, bn), lambda j: (0, j)),
      out_shape=jax.ShapeDtypeStruct((8, n), F32),
      compiler_params=pltpu.CompilerParams(
          dimension_semantics=("arbitrary",), vmem_limit_bytes=VMEM_LIMIT_V7X),
      name="adaln_mod",
  )(cvec, w_ada, b_ada)


N_BLOCKS = CHUNK // DIAG
N_UP = N_BLOCKS.bit_length() - 1


def _aligned(row):
  return row if isinstance(row, int) else pl.multiple_of(row, CHUNK)


def _query_blocks(level, dirn):
  return [blk for blk in range(N_BLOCKS) if ((blk >> (level - 1)) & 1) != dirn]


def _level_row(level, block, dirn):
  half = DIAG << (level - 1)
  start = (block >> level) * 2 * half
  return start + (half - 1 if dirn == 0 else half)


def _mix_body(*refs, layer, n_seq, seq_len, grid2d, has_state, emit_state):
  (x_ref, mod_ref, lbl_ref, n1_ref, win_ref, ng_ref, wa_ref, pw_ref, ps_ref, wb_ref,
   wo_ref) = refs[:11]
  rest = list(refs[11:])
  s0_ref = rest.pop(0) if has_state else None
  out_ref = rest.pop(0)
  st_ref = rest.pop(0) if emit_state else None
  (h_s, q_s, zf_s, v_s, og_s, zp_s, ga_s, gb_s, b_c, k_c, o_s, pm_s, st_s, tri_s, lev_s,
   band_s, icnt_s, tab_s) = rest[:18]
  cp_s = rest[18] if grid2d else None

  tt = n_seq * seq_len
  cps = seq_len // CHUNK

  ti = lax.broadcasted_iota(jnp.int32, (CHUNK, CHUNK), 0)
  si = lax.broadcasted_iota(jnp.int32, (CHUNK, CHUNK), 1)
  xr = ti ^ si
  lvl = jnp.zeros((CHUNK, CHUNK), jnp.int32)
  for j in range(1, N_UP + 1):
    lvl = jnp.where(xr >= (DIAG << (j - 1)), j, lvl)
  lev_s[0] = jnp.where(ti >= si, lvl, -1)
  lev_s[1] = jnp.where(ti <= si, lvl, -1)
  tri_s[0] = (ti >= si).astype(BF16)
  tri_s[1] = (ti <= si).astype(BF16)

  seg_shift = (GRID_W if grid2d else ROWS).bit_length() - 1
  tr = lax.broadcasted_iota(jnp.int32, (ROWS, ROWS), 0)
  sr = lax.broadcasted_iota(jnp.int32, (ROWS, ROWS), 1)
  same_seg = (tr >> seg_shift) == (sr >> seg_shift)
  for g, w in enumerate(POOL_WINDOWS):
    dlt = sr - tr
    band = (same_seg & (dlt >= -(w // 2)) & (dlt <= w // 2 - 1)).astype(F32)
    band_s[g] = band.astype(BF16)
    icnt_s[g] = 1.0 / jnp.sum(band, axis=1, keepdims=True)

  mod = mod_ref[...]
  sh1 = mod[:, 0:D_MODEL]
  a1 = n1_ref[...] * (1.0 + mod[:, D_MODEL:2 * D_MODEL])
  gate1 = mod[:, 2 * D_MODEL:3 * D_MODEL]
  lg = lbl_ref[...]
  e = jnp.exp(lg - jnp.max(lg, axis=0, keepdims=True))
  lb = jnp.sum(e[:layer + 1], axis=0) / jnp.sum(e, axis=0)

  for i in range(tt // ROWS):
    r = slice(i * ROWS, (i + 1) * ROWS)
    h_s[r, :] = (_rms_rows(x_ref[r, :]) * a1 + sh1).astype(BF16)
  hfull = h_s[...]

  def proj(c0, c1):
    return _dot(hfull, win_ref[:, c0:c1])

  q_s[...] = proj(0, D_A)
  zf_s[:, 0:D_A] = proj(D_A, 2 * D_A)
  zf_s[:, D_A:2 * D_A] = proj(2 * D_A, 3 * D_A)
  v_s[...] = proj(3 * D_A, 4 * D_A).astype(BF16)
  zo = proj(4 * D_A, 5 * D_A)
  og_s[...] = _silu(zo).astype(BF16)
  zp_s[...] = proj(5 * D_A, 5 * D_A + D_B)
  c0 = 5 * D_A + D_B
  for j in range(D_MODEL // 512):
    ga_s[:, j * 512:(j + 1) * 512] = _sigmoid(proj(c0 + j * 512, c0 + (j + 1) * 512)).astype(BF16)
  c0 += D_MODEL
  for j in range(D_MODEL // 512):
    gb_s[:, j * 512:(j + 1) * 512] = _sigmoid(proj(c0 + j * 512, c0 + (j + 1) * 512)).astype(BF16)

  half_c = 0.5 * (1.0 - lb)
  blk_i = lax.broadcasted_iota(jnp.int32, (N_BLOCKS, HEAD), 0)
  zero_tile = jnp.zeros((DIAG, HEAD), BF16)
  t_eb, t_et = 2 * N_UP, 2 * N_UP + 1

  def chunk(base, dirn, zero_state):
    rows = pl.ds(base, CHUNK)
    c1 = half_c[dirn:dirn + 1, :]
    th = jnp.tanh(0.5 * zf_s[rows, dirn * D_A:(dirn + 1) * D_A])
    ghi, glo = _split_bf16(jnp.log((1.0 - c1) + c1 * th))
    tri = tri_s[dirn]
    bsum = _dot(tri, ghi) + _dot(tri, glo)
    for hd in range(N_HEADS):
      b_c[dirn, hd] = bsum[:, hd * HEAD:(hd + 1) * HEAD]
    k_c[dirn] = c1 * (1.0 - th)
    crow = DIAG // 2 if dirn == 0 else DIAG // 2 - 1
    tot_row = CHUNK - 1 if dirn == 0 else 0
    for hd in range(N_HEADS):
      cs = slice(hd * HEAD, (hd + 1) * HEAD)
      brow = lambda r, hd=hd: b_c[dirn, hd, r:r + 1, :]
      tot = brow(tot_row)
      beta = b_c[dirn, hd, pl.ds(crow, N_BLOCKS, stride=DIAG), :]
      for l in range(1, N_UP + 1):
        n_par = N_BLOCKS >> l
        ref = brow(_level_row(l, (n_par - 1) << l, dirn))
        for p in range(n_par - 2, -1, -1):
          ref = jnp.where(blk_i < ((p + 1) << l), brow(_level_row(l, p << l, dirn)), ref)
        d = beta - ref
        tab_s[dirn, hd, l - 1] = jnp.exp(jnp.minimum(d, 0.0))
        tab_s[dirn, hd, N_UP + l - 1] = jnp.exp(jnp.minimum(-d, 0.0))
      tab_s[dirn, hd, t_eb] = jnp.exp(beta)
      tab_s[dirn, hd, t_et] = jnp.exp(tot - beta)
      trow = lambda t, blk, hd=hd: jnp.broadcast_to(tab_s[dirn, hd, t, blk:blk + 1, :],
                                                    (DIAG, HEAD))
      blk_rows = lambda a, blk: a[blk * DIAG:(blk + 1) * DIAG]

      beta_b = jnp.concatenate(
          [jnp.broadcast_to(brow(blk * DIAG + crow), (DIAG, HEAD)) for blk in range(N_BLOCKS)],
          axis=0)
      de = b_c[dirn, hd] - beta_b
      a0 = q_s[rows, cs] * jnp.exp(jnp.minimum(de, EXP_CLAMP))
      b0 = k_c[dirn, :, cs] * jnp.exp(jnp.minimum(-de, EXP_CLAMP))
      vh = v_s[rows, cs]
      p_same = _dot_nt(a0.astype(BF16), b0.astype(BF16))
      p_lev = {}
      for l in range(1, N_UP + 1):
        qb = _query_blocks(l, dirn)
        lhs = jnp.concatenate(
            [(blk_rows(a0, blk) * trow(l - 1, blk)).astype(BF16) for blk in qb], axis=0)
        rhs = jnp.concatenate(
            [zero_tile if blk in qb else (blk_rows(b0, blk) * trow(N_UP + l - 1, blk)).astype(BF16)
             for blk in range(N_BLOCKS)], axis=0)
        p_lev[l] = _dot_nt(lhs, rhs)
      sc_rows = []
      for blk in range(N_BLOCKS):
        lev = lev_s[dirn, blk * DIAG:(blk + 1) * DIAG, :]
        r = jnp.where(lev == 0, blk_rows(p_same, blk), 0.0)
        for l in range(1, N_UP + 1):
          qb = _query_blocks(l, dirn)
          if blk in qb:
            r = jnp.where(lev == l, blk_rows(p_lev[l], qb.index(blk)), r)
        sc_rows.append(r.astype(BF16))
      o = _dot(jnp.concatenate(sc_rows, axis=0), vh)
      kt = jnp.concatenate(
          [(blk_rows(b0, blk) * trow(t_et, blk)).astype(BF16) for blk in range(N_BLOCKS)], axis=0)
      if zero_state:
        st_s[dirn, hd] = _dot_tn(vh, kt)
      else:
        qt = jnp.concatenate(
            [(blk_rows(a0, blk) * trow(t_eb, blk)).astype(BF16) for blk in range(N_BLOCKS)], axis=0)
        st = st_s[dirn, hd]
        o = o + _dot_nt(qt, st.astype(BF16))
        st_s[dirn, hd] = st * jnp.exp(tot) + _dot_tn(vh, kt)
      o_s[dirn, rows, cs] = o

  def scan_sequence(seq_base):
    if has_state:
      for dirn in range(2):
        for hd in range(N_HEADS):
          st_s[dirn, hd] = s0_ref[dirn, hd].T

      def body(ci, carry):
        chunk(_aligned(seq_base + ci * CHUNK), 0, False)
        chunk(_aligned(seq_base + (cps - 1 - ci) * CHUNK), 1, False)
        return carry

      lax.fori_loop(0, cps, body, 0)
    else:
      for ci in range(cps):
        chunk(_aligned(seq_base + ci * CHUNK), 0, ci == 0)
        chunk(_aligned(seq_base + (cps - 1 - ci) * CHUNK), 1, ci == 0)

  if n_seq == 1:
    scan_sequence(0)
    if emit_state:
      for dirn in range(2):
        for hd in range(N_HEADS):
          st_ref[0, dirn, hd] = st_s[dirn, hd].T
  else:
    def seq_body(sq, carry):
      scan_sequence(sq * seq_len)
      if emit_state:
        for dirn in range(2):
          for hd in range(N_HEADS):
            st_ref[sq, dirn, hd] = st_s[dirn, hd].T
      return carry

    lax.fori_loop(0, n_seq, seq_body, 0)

  def band_pool(r, g):
    gs = slice(g * GROUP, (g + 1) * GROUP)
    zhi, zlo = _split_bf16(zp_s[r, gs])
    return (_dot(band_s[g], zhi) + _dot(band_s[g], zlo)) * icnt_s[g]

  if not grid2d:
    for i in range(tt // ROWS):
      r = slice(i * ROWS, (i + 1) * ROWS)
      for g in range(N_GROUPS):
        gs = slice(g * GROUP, (g + 1) * GROUP)
        pm_s[r, gs] = (band_pool(r, g) - zp_s[r, gs]).astype(BF16)
  else:
    n_rows = seq_len // GRID_W
    for i in range(tt // ROWS):
      r = slice(i * ROWS, (i + 1) * ROWS)
      for g in range(N_GROUPS):
        cp_s[r, g * GROUP:(g + 1) * GROUP] = band_pool(r, g)
    for g, w in enumerate(POOL_WINDOWS):
      gs = slice(g * GROUP, (g + 1) * GROUP)
      acc = None
      lo_prev = hi_prev = None
      for rr in range(n_rows):
        lo = max(rr - w // 2, 0)
        hi = min(rr + w // 2 - 1, n_rows - 1)
        if rr == 0:
          acc = cp_s[0:GRID_W, gs]
          for k in range(1, hi + 1):
            acc = acc + cp_s[k * GRID_W:(k + 1) * GRID_W, gs]
        else:
          if hi > hi_prev:
            acc = acc + cp_s[hi * GRID_W:(hi + 1) * GRID_W, gs]
          if lo > lo_prev:
            acc = acc - cp_s[lo_prev * GRID_W:(lo_prev + 1) * GRID_W, gs]
        lo_prev, hi_prev = lo, hi
        rs = slice(rr * GRID_W, (rr + 1) * GRID_W)
        pm_s[rs, gs] = (acc * (1.0 / (hi - lo + 1)) - zp_s[rs, gs]).astype(BF16)

  ng = ng_ref[...]
  psc = ps_ref[...]
  for i in range(tt // ROWS):
    r = slice(i * ROWS, (i + 1) * ROWS)
    parts = []
    for hd in range(N_HEADS):
      cs = slice(hd * HEAD, (hd + 1) * HEAD)
      parts.append(_rms_rows(o_s[0, r, cs] + o_s[1, r, cs]) * ng[:, cs])
    oa = (jnp.concatenate(parts, axis=1) * og_s[r, :].astype(F32)).astype(BF16)
    ya = _dot(oa, wa_ref[...])
    obp = []
    for g in range(N_GROUPS):
      gs = slice(g * GROUP, (g + 1) * GROUP)
      obp.append(_dot(pm_s[r, gs], pw_ref[g]) * psc[:, gs])
    ob = jnp.concatenate(obp, axis=1).astype(BF16)
    yb = _dot(ob, wb_ref[...])
    merged = (ga_s[r, :].astype(F32) * ya + gb_s[r, :].astype(F32) * yb).astype(BF16)
    out_ref[r, :] = x_ref[r, :] + gate1 * _dot(merged, wo_ref[...])


def _const_spec(shape):
  nd = len(shape)
  return pl.BlockSpec(shape, lambda i, _nd=nd: (0,) * _nd, pipeline_mode=pl.Buffered(1))


def _mix_call(x2d, mod3, mod_row, lbl, n1, win, ng, wa, pw, ps, wb, wo, s0, *, layer, n_seq,
              seq_len, grid2d, emit_state):
  tokens = x2d.shape[0]
  tt = n_seq * seq_len
  assert tokens % tt == 0 and seq_len % CHUNK == 0 and tt % ROWS == 0
  steps = tokens // tt
  has_state = s0 is not None
  x_mode = pl.Buffered(1) if steps <= 2 else None
  in_specs = [
      pl.BlockSpec((tt, D_MODEL), lambda i: (i, 0), pipeline_mode=x_mode),
      pl.BlockSpec((None, 1, 6 * D_MODEL), lambda i: (mod_row(i), 0, 0)),
      _const_spec(lbl.shape), _const_spec(n1.shape), _const_spec(win.shape), _const_spec(ng.shape),
      _const_spec(wa.shape), _const_spec(pw.shape), _const_spec(ps.shape), _const_spec(wb.shape),
      _const_spec(wo.shape),
  ]
  args = [x2d, mod3, lbl, n1, win, ng, wa, pw, ps, wb, wo]
  if has_state:
    in_specs.append(pl.BlockSpec((None, None, 2, N_HEADS, HEAD, HEAD),
                                 lambda i: (i, layer, 0, 0, 0, 0)))
    args.append(s0)
  out_specs = [pl.BlockSpec((tt, D_MODEL), lambda i: (i, 0))]
  out_shape = [jax.ShapeDtypeStruct((tokens, D_MODEL), F32)]
  if emit_state:
    out_specs.append(pl.BlockSpec((n_seq, 2, N_HEADS, HEAD, HEAD), lambda i: (i, 0, 0, 0, 0)))
    out_shape.append(jax.ShapeDtypeStruct((tokens // seq_len, 2, N_HEADS, HEAD, HEAD), F32))
  scratch = [
      pltpu.VMEM((tt, D_MODEL), BF16),
      pltpu.VMEM((tt, D_A), F32),
      pltpu.VMEM((tt, 2 * D_A), F32),
      pltpu.VMEM((tt, D_A), BF16),
      pltpu.VMEM((tt, D_A), BF16),
      pltpu.VMEM((tt, D_B), F32),
      pltpu.VMEM((tt, D_MODEL), BF16),
      pltpu.VMEM((tt, D_MODEL), BF16),
      pltpu.VMEM((2, N_HEADS, CHUNK, HEAD), F32),
      pltpu.VMEM((2, CHUNK, D_A), F32),
      pltpu.VMEM((2, tt, D_A), F32),
      pltpu.VMEM((tt, D_B), BF16),
      pltpu.VMEM((2, N_HEADS, HEAD, HEAD), F32),
      pltpu.VMEM((2, CHUNK, CHUNK), BF16),
      pltpu.VMEM((2, CHUNK, CHUNK), jnp.int32),
      pltpu.VMEM((N_GROUPS, ROWS, ROWS), BF16),
      pltpu.VMEM((N_GROUPS, ROWS, 1), F32),
      pltpu.VMEM((2, N_HEADS, 2 * N_UP + 2, N_BLOCKS, HEAD), F32),
  ]
  if grid2d:
    scratch.append(pltpu.VMEM((tt, D_B), F32))
  body = functools.partial(_mix_body, layer=layer, n_seq=n_seq, seq_len=seq_len, grid2d=grid2d,
                           has_state=has_state, emit_state=emit_state)
  return pl.pallas_call(
      body,
      grid=(steps,),
      in_specs=in_specs,
      out_specs=out_specs,
      out_shape=out_shape,
      scratch_shapes=scratch,
      compiler_params=pltpu.CompilerParams(
          dimension_semantics=("arbitrary",), vmem_limit_bytes=VMEM_LIMIT_V7X),
      name="mix_grid2d" if grid2d else "mix_seq1d",
  )(*args)


FFN_COLS = 256


def _ffn_body(x_ref, mod_ref, n2_ref, wi_ref, wo_ref, fg_ref, o_ref, h_s, hid_s):
  tt = x_ref.shape[0]
  mod = mod_ref[...]
  sh2 = mod[:, 3 * D_MODEL:4 * D_MODEL]
  a2 = n2_ref[...] * (1.0 + mod[:, 4 * D_MODEL:5 * D_MODEL])
  gate2 = mod[:, 5 * D_MODEL:6 * D_MODEL]
  for i in range(tt // ROWS):
    r = slice(i * ROWS, (i + 1) * ROWS)
    h_s[r, :] = (_rms_rows(x_ref[r, :]) * a2 + sh2).astype(BF16)
  h = h_s[...]
  for j in range(D_FF // FFN_COLS):
    cg = slice(j * FFN_COLS, (j + 1) * FFN_COLS)
    cu = slice(D_FF + j * FFN_COLS, D_FF + (j + 1) * FFN_COLS)
    gp = _dot(h, wi_ref[:, cg])
    up = _dot(h, wi_ref[:, cu])
    hid_s[:, cg] = (_silu(gp) * up).astype(BF16)
  o_ref[...] = _dot(hid_s[...], wo_ref[...])
  fg = fg_ref[...]
  for i in range(tt // ROWS):
    r = slice(i * ROWS, (i + 1) * ROWS)
    o_ref[r, :] = _rms_rows(x_ref[r, :] + gate2 * o_ref[r, :]) * fg


def _ffn_call(x2d, mod3, mod_row, n2, wi, wo, fg, *, tt):
  tokens = x2d.shape[0]
  assert tokens % tt == 0 and tt % ROWS == 0 and D_FF % FFN_COLS == 0
  return pl.pallas_call(
      _ffn_body,
      grid=(tokens // tt,),
      in_specs=[
          pl.BlockSpec((tt, D_MODEL), lambda i: (i, 0)),
          pl.BlockSpec((None, 1, 6 * D_MODEL), lambda i: (mod_row(i), 0, 0)),
          _const_spec(n2.shape), _const_spec(wi.shape), _const_spec(wo.shape),
          _const_spec(fg.shape),
      ],
      out_specs=pl.BlockSpec((tt, D_MODEL), lambda i: (i, 0)),
      out_shape=jax.ShapeDtypeStruct((tokens, D_MODEL), F32),
      scratch_shapes=[pltpu.VMEM((tt, D_MODEL), BF16), pltpu.VMEM((tt, D_FF), BF16)],
      compiler_params=pltpu.CompilerParams(
          dimension_semantics=("arbitrary",), vmem_limit_bytes=VMEM_LIMIT_V7X),
      name="ffn_final",
  )(x2d, mod3, n2, wi, wo, fg)


def kernel(x_prompt, x_sample, state_hgrn, c, c_ctx, w_ada, b_ada, norm1_g, w_in, hgrn_lb_logits,
           hgrn_norm_g, w_branch_a, pool_w, pool_scale, w_branch_b, w_out, norm2_g, w_ffn_in,
           w_ffn_out, final_g):
  bp, sp, _ = x_prompt.shape
  bs, ss, _ = x_sample.shape
  depth = w_in.shape[0]
  assert bs + 1 <= 8 and ss % GRID_W == 0
  xp = x_prompt.reshape(bp * sp, D_MODEL)
  xs = x_sample.reshape(bs * ss, D_MODEL)
  cvec = jnp.concatenate([c_ctx[None, :], c, jnp.zeros((8 - 1 - bs, D_MODEL), F32)], axis=0)
  row2 = lambda a: a.reshape(1, -1)
  ffn_tt = 512
  ctx_states = []
  for l in range(depth):
    mod3 = _mod_call(cvec, w_ada[l], row2(b_ada[l]))[:1 + bs, None, :]
    mix_w = (hgrn_lb_logits, row2(norm1_g[l]), w_in[l].astype(BF16), row2(hgrn_norm_g[l]),
             w_branch_a[l].astype(BF16), pool_w[l].astype(BF16), row2(pool_scale[l]),
             w_branch_b[l].astype(BF16), w_out[l].astype(BF16))
    ffn_w = (row2(norm2_g[l]), w_ffn_in[l].astype(BF16), w_ffn_out[l].astype(BF16))
    last = l == depth - 1
    fg = row2(final_g) if last else jnp.ones((1, D_MODEL), F32)
    xp, st = _mix_call(xp, mod3, lambda i: 0, *mix_w, None, layer=l, n_seq=2, seq_len=sp,
                       grid2d=False, emit_state=True)
    ctx_states.append(st)
    (xs,) = _mix_call(xs, mod3, lambda i: 1 + i, *mix_w, state_hgrn, layer=l, n_seq=1,
                      seq_len=ss, grid2d=True, emit_state=False)
    assert last, "the FFN call also applies the final norm"
    xp = _ffn_call(xp, mod3, lambda i: 0, *ffn_w, fg, tt=ffn_tt)
    xs = _ffn_call(xs, mod3, lambda i: 1 + i // (ss // ffn_tt), *ffn_w, fg, tt=ffn_tt)
  new_state = jnp.stack(ctx_states, axis=1).astype(x_prompt.dtype)
  return (xp.reshape(bp, sp, D_MODEL), xs.reshape(bs, ss, D_MODEL), new_state)
```

```python
import functools

import jax
import jax.numpy as jnp
from jax import lax
from jax.experimental import pallas as pl
from jax.experimental.pallas import tpu as pltpu

F32 = jnp.float32
BF16 = jnp.bfloat16

D_MODEL = 1024
D_A = 512
N_HEADS = 4
HEAD = D_A // N_HEADS
D_B = 512
N_GROUPS = 4
GROUP = D_B // N_GROUPS
POOL_WINDOWS = (2, 4, 8, 16)
GRID_W = 64
D_FF = 2816
IN_COLS = 5 * D_A + D_B + 2 * D_MODEL
EPS = 1e-6

CHUNK = 128
DIAG = 16
CHUNK_SLOTS = 2
ROWS = 256
EXP_CLAMP = 80.0
VMEM_LIMIT_V7X = 58 * 1024 * 1024


def _sigmoid(x):
  return 0.5 + 0.5 * jnp.tanh(0.5 * x)


def _silu(x):
  h = 0.5 * x
  return h + h * jnp.tanh(h)


def _dot(a, b):
  return jnp.dot(a, b, preferred_element_type=F32)


def _dot_nt(a, b):
  return lax.dot_general(a, b, (((1,), (1,)), ((), ())), preferred_element_type=F32)


def _dot_tn(a, b):
  return lax.dot_general(a, b, (((0,), (0,)), ((), ())), preferred_element_type=F32)


def _split_bf16(x):
  hi = x.astype(BF16)
  lo = (x - hi.astype(F32)).astype(BF16)
  return hi, lo


def _rms_rows(x):
  return x * lax.rsqrt(jnp.mean(x * x, axis=-1, keepdims=True) + EPS)


def _mod_body(c_ref, w_ref, b_ref, o_ref):
  cv = c_ref[...]
  s = _silu(cv).astype(BF16)
  o_ref[...] = _dot(s, w_ref[...].astype(BF16)) + b_ref[...]


def _mod_call(cvec, w_ada, b_ada):
  n = w_ada.shape[1]
  bn = 1536
  return pl.pallas_call(
      _mod_body,
      grid=(n // bn,),
      in_specs=[
          pl.BlockSpec((8, D_MODEL), lambda j: (0, 0)),
          pl.BlockSpec((D_MODEL, bn), lambda j: (0, j)),
          pl.BlockSpec((1, bn), lambda j: (0, j)),
      ],
      out_specs=pl.BlockSpec((8, bn), lambda j: (0, j)),
      out_shape=jax.ShapeDtypeStruct((8, n), F32),
      compiler_params=pltpu.CompilerParams(
          dimension_semantics=("arbitrary",), vmem_limit_bytes=VMEM_LIMIT_V7X),
      name="adaln_mod",
  )(cvec, w_ada, b_ada)


N_BLOCKS = CHUNK // DIAG
N_UP = N_BLOCKS.bit_length() - 1


def _aligned(row):
  return row if isinstance(row, int) else pl.multiple_of(row, CHUNK)


def _query_blocks(level, dirn):
  return [blk for blk in range(N_BLOCKS) if ((blk >> (level - 1)) & 1) != dirn]


def _level_row(level, block, dirn):
  half = DIAG << (level - 1)
  start = (block >> level) * 2 * half
  return start + (half - 1 if dirn == 0 else half)


def _mix_body(*refs, layer, n_seq, seq_len, grid2d, has_state, emit_state):
  (x_ref, mod_ref, lbl_ref, n1_ref, win_ref, ng_ref, wa_ref, pw_ref, ps_ref, wb_ref,
   wo_ref) = refs[:11]
  rest = list(refs[11:])
  s0_ref = rest.pop(0) if has_state else None
  out_ref = rest.pop(0)
  st_ref = rest.pop(0) if emit_state else None
  (h_s, q_s, zf_s, v_s, og_s, zp_s, ga_s, gb_s, b_c, k_c, o_s, pm_s, st_s, tri_s, lev_s,
   band_s, icnt_s, tab_s) = rest[:18]
  cp_s = rest[18] if grid2d else None

  tt = n_seq * seq_len
  cps = seq_len // CHUNK

  ti = lax.broadcasted_iota(jnp.int32, (CHUNK, CHUNK), 0)
  si = lax.broadcasted_iota(jnp.int32, (CHUNK, CHUNK), 1)
  xr = ti ^ si
  lvl = jnp.zeros((CHUNK, CHUNK), jnp.int32)
  for j in range(1, N_UP + 1):
    lvl = jnp.where(xr >= (DIAG << (j - 1)), j, lvl)
  lev_s[0] = jnp.where(ti >= si, lvl, -1)
  lev_s[1] = jnp.where(ti <= si, lvl, -1)
  tri_s[0] = (ti >= si).astype(BF16)
  tri_s[1] = (ti <= si).astype(BF16)

  seg_shift = (GRID_W if grid2d else ROWS).bit_length() - 1
  tr = lax.broadcasted_iota(jnp.int32, (ROWS, ROWS), 0)
  sr = lax.broadcasted_iota(jnp.int32, (ROWS, ROWS), 1)
  same_seg = (tr >> seg_shift) == (sr >> seg_shift)
  for g, w in enumerate(POOL_WINDOWS):
    dlt = sr - tr
    band = (same_seg & (dlt >= -(w // 2)) & (dlt <= w // 2 - 1)).astype(F32)
    band_s[g] = band.astype(BF16)
    icnt_s[g] = 1.0 / jnp.sum(band, axis=1, keepdims=True)

  mod = mod_ref[...]
  sh1 = mod[:, 0:D_MODEL]
  a1 = n1_ref[...] * (1.0 + mod[:, D_MODEL:2 * D_MODEL])
  gate1 = mod[:, 2 * D_MODEL:3 * D_MODEL]
  lg = lbl_ref[...]
  e = jnp.exp(lg - jnp.max(lg, axis=0, keepdims=True))
  lb = jnp.sum(e[:layer + 1], axis=0) / jnp.sum(e, axis=0)

  for i in range(tt // ROWS):
    r = slice(i * ROWS, (i + 1) * ROWS)
    h_s[r, :] = (_rms_rows(x_ref[r, :]) * a1 + sh1).astype(BF16)
  hfull = h_s[...]

  def proj(c0, c1):
    return _dot(hfull, win_ref[:, c0:c1])

  q_s[...] = proj(0, D_A)
  zf_s[:, 0:D_A] = proj(D_A, 2 * D_A)
  zf_s[:, D_A:2 * D_A] = proj(2 * D_A, 3 * D_A)
  v_s[...] = proj(3 * D_A, 4 * D_A).astype(BF16)
  zo = proj(4 * D_A, 5 * D_A)
  og_s[...] = _silu(zo).astype(BF16)
  zp_s[...] = proj(5 * D_A, 5 * D_A + D_B)
  c0 = 5 * D_A + D_B
  for j in range(D_MODEL // 512):
    ga_s[:, j * 512:(j + 1) * 512] = _sigmoid(proj(c0 + j * 512, c0 + (j + 1) * 512)).astype(BF16)
  c0 += D_MODEL
  for j in range(D_MODEL // 512):
    gb_s[:, j * 512:(j + 1) * 512] = _sigmoid(proj(c0 + j * 512, c0 + (j + 1) * 512)).astype(BF16)

  half_c = 0.5 * (1.0 - lb)
  blk_i = lax.broadcasted_iota(jnp.int32, (N_BLOCKS, HEAD), 0)
  zero_tile = jnp.zeros((DIAG, HEAD), BF16)
  t_eb, t_et = 2 * N_UP, 2 * N_UP + 1

  blk_rows = lambda a, blk: a[blk * DIAG:(blk + 1) * DIAG]

  def stage_gates(base, dirn, slot):
    sd = 2 * slot + dirn
    c1 = half_c[dirn:dirn + 1, :]
    th = jnp.tanh(0.5 * zf_s[pl.ds(base, CHUNK), dirn * D_A:(dirn + 1) * D_A])
    ghi, glo = _split_bf16(jnp.log((1.0 - c1) + c1 * th))
    tri = tri_s[dirn]
    bsum = _dot(tri, ghi) + _dot(tri, glo)
    for hd in range(N_HEADS):
      b_c[sd, hd] = bsum[:, hd * HEAD:(hd + 1) * HEAD]
    k_c[sd] = c1 * (1.0 - th)

  def stage_factors(base, dirn, zero_state, slot, hd):
    sd = 2 * slot + dirn
    cs = slice(hd * HEAD, (hd + 1) * HEAD)
    crow = DIAG // 2 if dirn == 0 else DIAG // 2 - 1
    tot_row = CHUNK - 1 if dirn == 0 else 0
    brow = lambda r: b_c[sd, hd, r:r + 1, :]
    tot = brow(tot_row)
    beta = b_c[sd, hd, pl.ds(crow, N_BLOCKS, stride=DIAG), :]
    for l in range(1, N_UP + 1):
      n_par = N_BLOCKS >> l
      ref = brow(_level_row(l, (n_par - 1) << l, dirn))
      for p in range(n_par - 2, -1, -1):
        ref = jnp.where(blk_i < ((p + 1) << l), brow(_level_row(l, p << l, dirn)), ref)
      d = beta - ref
      tab_s[sd, hd, l - 1] = jnp.exp(jnp.minimum(d, 0.0))
      tab_s[sd, hd, N_UP + l - 1] = jnp.exp(jnp.minimum(-d, 0.0))
    tab_s[sd, hd, t_eb] = jnp.exp(beta)
    tab_s[sd, hd, t_et] = jnp.exp(tot - beta)
    trow = lambda t, blk: jnp.broadcast_to(tab_s[sd, hd, t, blk:blk + 1, :], (DIAG, HEAD))

    beta_b = jnp.concatenate(
        [jnp.broadcast_to(brow(blk * DIAG + crow), (DIAG, HEAD)) for blk in range(N_BLOCKS)],
        axis=0)
    de = b_c[sd, hd] - beta_b
    a0 = q_s[pl.ds(base, CHUNK), cs] * jnp.exp(jnp.minimum(de, EXP_CLAMP))
    b0 = k_c[sd, :, cs] * jnp.exp(jnp.minimum(-de, EXP_CLAMP))
    ops = dict(base=base, dirn=dirn, hd=hd, zero_state=zero_state, tot=tot,
               a0=a0.astype(BF16), b0=b0.astype(BF16), lhs={}, rhs={})
    for l in range(1, N_UP + 1):
      qb = _query_blocks(l, dirn)
      ops["lhs"][l] = jnp.concatenate(
          [(blk_rows(a0, blk) * trow(l - 1, blk)).astype(BF16) for blk in qb], axis=0)
      ops["rhs"][l] = jnp.concatenate(
          [zero_tile if blk in qb else (blk_rows(b0, blk) * trow(N_UP + l - 1, blk)).astype(BF16)
           for blk in range(N_BLOCKS)], axis=0)
    ops["kt"] = jnp.concatenate(
        [(blk_rows(b0, blk) * trow(t_et, blk)).astype(BF16) for blk in range(N_BLOCKS)], axis=0)
    if not zero_state:
      ops["qt"] = jnp.concatenate(
          [(blk_rows(a0, blk) * trow(t_eb, blk)).astype(BF16) for blk in range(N_BLOCKS)], axis=0)
    return ops

  def stage_scores(ops):
    dirn = ops["dirn"]
    p_same = _dot_nt(ops["a0"], ops["b0"])
    p_lev = {l: _dot_nt(ops["lhs"][l], ops["rhs"][l]) for l in range(1, N_UP + 1)}
    sc_rows = []
    for blk in range(N_BLOCKS):
      lev = lev_s[dirn, blk * DIAG:(blk + 1) * DIAG, :]
      r = jnp.where(lev == 0, blk_rows(p_same, blk), 0.0)
      for l in range(1, N_UP + 1):
        qb = _query_blocks(l, dirn)
        if blk in qb:
          r = jnp.where(lev == l, blk_rows(p_lev[l], qb.index(blk)), r)
      sc_rows.append(r.astype(BF16))
    ops["sc"] = jnp.concatenate(sc_rows, axis=0)

  def stage_out(ops):
    dirn, hd = ops["dirn"], ops["hd"]
    rows = pl.ds(ops["base"], CHUNK)
    cs = slice(hd * HEAD, (hd + 1) * HEAD)
    vh = v_s[rows, cs]
    o = _dot(ops["sc"], vh)
    if ops["zero_state"]:
      st_s[dirn, hd] = _dot_tn(vh, ops["kt"])
    else:
      st = st_s[dirn, hd]
      o = o + _dot_nt(ops["qt"], st.astype(BF16))
      st_s[dirn, hd] = st * jnp.exp(ops["tot"]) + _dot_tn(vh, ops["kt"])
    o_s[dirn, rows, cs] = o

  def scan_group(chunks):
    for base, dirn, _, slot in chunks:
      stage_gates(base, dirn, slot)
    group = [stage_factors(base, dirn, zero_state, slot, hd)
             for base, dirn, zero_state, slot in chunks for hd in range(N_HEADS)]
    for ops in group:
      stage_scores(ops)
    for ops in group:
      stage_out(ops)

  def chunk_pair(seq_base, ci, zero_state, slot):
    return [(_aligned(seq_base + ci * CHUNK), 0, zero_state, slot),
            (_aligned(seq_base + (cps - 1 - ci) * CHUNK), 1, zero_state, slot)]

  def scan_sequence(seq_base):
    if has_state:
      for dirn in range(2):
        for hd in range(N_HEADS):
          st_s[dirn, hd] = s0_ref[dirn, hd].T

      def body(cj, carry):
        scan_group([c for slot in range(CHUNK_SLOTS)
                    for c in chunk_pair(seq_base, cj * CHUNK_SLOTS + slot, False, slot)])
        return carry

      lax.fori_loop(0, cps // CHUNK_SLOTS, body, 0)
    else:
      assert cps <= CHUNK_SLOTS
      scan_group([c for ci in range(cps) for c in chunk_pair(seq_base, ci, ci == 0, ci)])

  if n_seq == 1:
    scan_sequence(0)
    if emit_state:
      for dirn in range(2):
        for hd in range(N_HEADS):
          st_ref[0, dirn, hd] = st_s[dirn, hd].T
  else:
    def seq_body(sq, carry):
      scan_sequence(sq * seq_len)
      if emit_state:
        for dirn in range(2):
          for hd in range(N_HEADS):
            st_ref[sq, dirn, hd] = st_s[dirn, hd].T
      return carry

    lax.fori_loop(0, n_seq, seq_body, 0)

  def band_pool(r, g):
    gs = slice(g * GROUP, (g + 1) * GROUP)
    zhi, zlo = _split_bf16(zp_s[r, gs])
    return (_dot(band_s[g], zhi) + _dot(band_s[g], zlo)) * icnt_s[g]

  if not grid2d:
    for i in range(tt // ROWS):
      r = slice(i * ROWS, (i + 1) * ROWS)
      for g in range(N_GROUPS):
        gs = slice(g * GROUP, (g + 1) * GROUP)
        pm_s[r, gs] = (band_pool(r, g) - zp_s[r, gs]).astype(BF16)
  else:
    n_rows = seq_len // GRID_W
    for i in range(tt // ROWS):
      r = slice(i * ROWS, (i + 1) * ROWS)
      for g in range(N_GROUPS):
        cp_s[r, g * GROUP:(g + 1) * GROUP] = band_pool(r, g)
    for g, w in enumerate(POOL_WINDOWS):
      gs = slice(g * GROUP, (g + 1) * GROUP)
      acc = None
      lo_prev = hi_prev = None
      for rr in range(n_rows):
        lo = max(rr - w // 2, 0)
        hi = min(rr + w // 2 - 1, n_rows - 1)
        if rr == 0:
          acc = cp_s[0:GRID_W, gs]
          for k in range(1, hi + 1):
            acc = acc + cp_s[k * GRID_W:(k + 1) * GRID_W, gs]
        else:
          if hi > hi_prev:
            acc = acc + cp_s[hi * GRID_W:(hi + 1) * GRID_W, gs]
          if lo > lo_prev:
            acc = acc - cp_s[lo_prev * GRID_W:(lo_prev + 1) * GRID_W, gs]
        lo_prev, hi_prev = lo, hi
        rs = slice(rr * GRID_W, (rr + 1) * GRID_W)
        pm_s[rs, gs] = (acc * (1.0 / (hi - lo + 1)) - zp_s[rs, gs]).astype(BF16)

  ng = ng_ref[...]
  psc = ps_ref[...]
  for i in range(tt // ROWS):
    r = slice(i * ROWS, (i + 1) * ROWS)
    parts = []
    for hd in range(N_HEADS):
      cs = slice(hd * HEAD, (hd + 1) * HEAD)
      parts.append(_rms_rows(o_s[0, r, cs] + o_s[1, r, cs]) * ng[:, cs])
    oa = (jnp.concatenate(parts, axis=1) * og_s[r, :].astype(F32)).astype(BF16)
    ya = _dot(oa, wa_ref[...])
    obp = []
    for g in range(N_GROUPS):
      gs = slice(g * GROUP, (g + 1) * GROUP)
      obp.append(_dot(pm_s[r, gs], pw_ref[g]) * psc[:, gs])
    ob = jnp.concatenate(obp, axis=1).astype(BF16)
    yb = _dot(ob, wb_ref[...])
    merged = (ga_s[r, :].astype(F32) * ya + gb_s[r, :].astype(F32) * yb).astype(BF16)
    out_ref[r, :] = x_ref[r, :] + gate1 * _dot(merged, wo_ref[...])


def _const_spec(shape):
  nd = len(shape)
  return pl.BlockSpec(shape, lambda i, _nd=nd: (0,) * _nd, pipeline_mode=pl.Buffered(1))


def _mix_call(x2d, mod3, mod_row, lbl, n1, win, ng, wa, pw, ps, wb, wo, s0, *, layer, n_seq,
              seq_len, grid2d, emit_state):
  tokens = x2d.shape[0]
  tt = n_seq * seq_len
  assert tokens % tt == 0 and seq_len % (CHUNK * CHUNK_SLOTS) == 0 and tt % ROWS == 0
  steps = tokens // tt
  has_state = s0 is not None
  x_mode = pl.Buffered(1) if steps <= 2 else None
  in_specs = [
      pl.BlockSpec((tt, D_MODEL), lambda i: (i, 0), pipeline_mode=x_mode),
      pl.BlockSpec((None, 1, 6 * D_MODEL), lambda i: (mod_row(i), 0, 0)),
      _const_spec(lbl.shape), _const_spec(n1.shape), _const_spec(win.shape), _const_spec(ng.shape),
      _const_spec(wa.shape), _const_spec(pw.shape), _const_spec(ps.shape), _const_spec(wb.shape),
      _const_spec(wo.shape),
  ]
  args = [x2d, mod3, lbl, n1, win, ng, wa, pw, ps, wb, wo]
  if has_state:
    in_specs.append(pl.BlockSpec((None, None, 2, N_HEADS, HEAD, HEAD),
                                 lambda i: (i, layer, 0, 0, 0, 0)))
    args.append(s0)
  out_specs = [pl.BlockSpec((tt, D_MODEL), lambda i: (i, 0))]
  out_shape = [jax.ShapeDtypeStruct((tokens, D_MODEL), F32)]
  if emit_state:
    out_specs.append(pl.BlockSpec((n_seq, 2, N_HEADS, HEAD, HEAD), lambda i: (i, 0, 0, 0, 0)))
    out_shape.append(jax.ShapeDtypeStruct((tokens // seq_len, 2, N_HEADS, HEAD, HEAD), F32))
  scratch = [
      pltpu.VMEM((tt, D_MODEL), BF16),
      pltpu.VMEM((tt, D_A), F32),
      pltpu.VMEM((tt, 2 * D_A), F32),
      pltpu.VMEM((tt, D_A), BF16),
      pltpu.VMEM((tt, D_A), BF16),
      pltpu.VMEM((tt, D_B), F32),
      pltpu.VMEM((tt, D_MODEL), BF16),
      pltpu.VMEM((tt, D_MODEL), BF16),
      pltpu.VMEM((2 * CHUNK_SLOTS, N_HEADS, CHUNK, HEAD), F32),
      pltpu.VMEM((2 * CHUNK_SLOTS, CHUNK, D_A), F32),
      pltpu.VMEM((2, tt, D_A), F32),
      pltpu.VMEM((tt, D_B), BF16),
      pltpu.VMEM((2, N_HEADS, HEAD, HEAD), F32),
      pltpu.VMEM((2, CHUNK, CHUNK), BF16),
      pltpu.VMEM((2, CHUNK, CHUNK), jnp.int32),
      pltpu.VMEM((N_GROUPS, ROWS, ROWS), BF16),
      pltpu.VMEM((N_GROUPS, ROWS, 1), F32),
      pltpu.VMEM((2 * CHUNK_SLOTS, N_HEADS, 2 * N_UP + 2, N_BLOCKS, HEAD), F32),
  ]
  if grid2d:
    scratch.append(pltpu.VMEM((tt, D_B), F32))
  body = functools.partial(_mix_body, layer=layer, n_seq=n_seq, seq_len=seq_len, grid2d=grid2d,
                           has_state=has_state, emit_state=emit_state)
  return pl.pallas_call(
      body,
      grid=(steps,),
      in_specs=in_specs,
      out_specs=out_specs,
      out_shape=out_shape,
      scratch_shapes=scratch,
      compiler_params=pltpu.CompilerParams(
          dimension_semantics=("arbitrary",), vmem_limit_bytes=VMEM_LIMIT_V7X),
      name="mix_grid2d" if grid2d else "mix_seq1d",
  )(*args)


FFN_COLS = 256


def _ffn_body(x_ref, mod_ref, n2_ref, wi_ref, wo_ref, fg_ref, o_ref, h_s, hid_s):
  tt = x_ref.shape[0]
  mod = mod_ref[...]
  sh2 = mod[:, 3 * D_MODEL:4 * D_MODEL]
  a2 = n2_ref[...] * (1.0 + mod[:, 4 * D_MODEL:5 * D_MODEL])
  gate2 = mod[:, 5 * D_MODEL:6 * D_MODEL]
  for i in range(tt // ROWS):
    r = slice(i * ROWS, (i + 1) * ROWS)
    h_s[r, :] = (_rms_rows(x_ref[r, :]) * a2 + sh2).astype(BF16)
  h = h_s[...]
  for j in range(D_FF // FFN_COLS):
    cg = slice(j * FFN_COLS, (j + 1) * FFN_COLS)
    cu = slice(D_FF + j * FFN_COLS, D_FF + (j + 1) * FFN_COLS)
    gp = _dot(h, wi_ref[:, cg])
    up = _dot(h, wi_ref[:, cu])
    hid_s[:, cg] = (_silu(gp) * up).astype(BF16)
  o_ref[...] = _dot(hid_s[...], wo_ref[...])
  fg = fg_ref[...]
  for i in range(tt // ROWS):
    r = slice(i * ROWS, (i + 1) * ROWS)
    o_ref[r, :] = _rms_rows(x_ref[r, :] + gate2 * o_ref[r, :]) * fg


def _ffn_call(x2d, mod3, mod_row, n2, wi, wo, fg, *, tt):
  tokens = x2d.shape[0]
  assert tokens % tt == 0 and tt % ROWS == 0 and D_FF % FFN_COLS == 0
  return pl.pallas_call(
      _ffn_body,
      grid=(tokens // tt,),
      in_specs=[
          pl.BlockSpec((tt, D_MODEL), lambda i: (i, 0)),
          pl.BlockSpec((None, 1, 6 * D_MODEL), lambda i: (mod_row(i), 0, 0)),
          _const_spec(n2.shape), _const_spec(wi.shape), _const_spec(wo.shape),
          _const_spec(fg.shape),
      ],
      out_specs=pl.BlockSpec((tt, D_MODEL), lambda i: (i, 0)),
      out_shape=jax.ShapeDtypeStruct((tokens, D_MODEL), F32),
      scratch_shapes=[pltpu.VMEM((tt, D_MODEL), BF16), pltpu.VMEM((tt, D_FF), BF16)],
      compiler_params=pltpu.CompilerParams(
          dimension_semantics=("arbitrary",), vmem_limit_bytes=VMEM_LIMIT_V7X),
      name="ffn_final",
  )(x2d, mod3, n2, wi, wo, fg)


def kernel(x_prompt, x_sample, state_hgrn, c, c_ctx, w_ada, b_ada, norm1_g, w_in, hgrn_lb_logits,
           hgrn_norm_g, w_branch_a, pool_w, pool_scale, w_branch_b, w_out, norm2_g, w_ffn_in,
           w_ffn_out, final_g):
  bp, sp, _ = x_prompt.shape
  bs, ss, _ = x_sample.shape
  depth = w_in.shape[0]
  assert bs + 1 <= 8 and ss % GRID_W == 0
  xp = x_prompt.reshape(bp * sp, D_MODEL)
  xs = x_sample.reshape(bs * ss, D_MODEL)
  cvec = jnp.concatenate([c_ctx[None, :], c, jnp.zeros((8 - 1 - bs, D_MODEL), F32)], axis=0)
  row2 = lambda a: a.reshape(1, -1)
  ffn_tt = 512
  ctx_states = []
  for l in range(depth):
    mod3 = _mod_call(cvec, w_ada[l], row2(b_ada[l]))[:1 + bs, None, :]
    mix_w = (hgrn_lb_logits, row2(norm1_g[l]), w_in[l].astype(BF16), row2(hgrn_norm_g[l]),
             w_branch_a[l].astype(BF16), pool_w[l].astype(BF16), row2(pool_scale[l]),
             w_branch_b[l].astype(BF16), w_out[l].astype(BF16))
    ffn_w = (row2(norm2_g[l]), w_ffn_in[l].astype(BF16), w_ffn_out[l].astype(BF16))
    last = l == depth - 1
    fg = row2(final_g) if last else jnp.ones((1, D_MODEL), F32)
    xp, st = _mix_call(xp, mod3, lambda i: 0, *mix_w, None, layer=l, n_seq=2, seq_len=sp,
                       grid2d=False, emit_state=True)
    ctx_states.append(st)
    (xs,) = _mix_call(xs, mod3, lambda i: 1 + i, *mix_w, state_hgrn, layer=l, n_seq=1,
                      seq_len=ss, grid2d=True, emit_state=False)
    assert last, "the FFN call also applies the final norm"
    xp = _ffn_call(xp, mod3, lambda i: 0, *ffn_w, fg, tt=ffn_tt)
    xs = _ffn_call(xs, mod3, lambda i: 1 + i // (ss // ffn_tt), *ffn_w, fg, tt=ffn_tt)
  new_state = jnp.stack(ctx_states, axis=1).astype(x_prompt.dtype)
  return (xp.reshape(bp, sp, D_MODEL), xs.reshape(bs, ss, D_MODEL), new_state)
```

```python
import functools

import jax
import jax.numpy as jnp
from jax import lax
from jax.experimental import pallas as pl
from jax.experimental.pallas import tpu as pltpu

F32 = jnp.float32
BF16 = jnp.bfloat16

D_MODEL = 1024
D_A = 512
N_HEADS = 4
HEAD = D_A // N_HEADS
D_B = 512
N_GROUPS = 4
GROUP = D_B // N_GROUPS
POOL_WINDOWS = (2, 4, 8, 16)
GRID_W = 64
D_FF = 2816
IN_COLS = 5 * D_A + D_B + 2 * D_MODEL
EPS = 1e-6

CHUNK = 128
DIAG = 16
CHUNK_SLOTS = 2
ROWS = 256
EXP_CLAMP = 80.0
VMEM_LIMIT_V7X = 58 * 1024 * 1024


def _sigmoid(x):
  return 0.5 + 0.5 * jnp.tanh(0.5 * x)


def _silu(x):
  h = 0.5 * x
  return h + h * jnp.tanh(h)


def _dot(a, b):
  return jnp.dot(a, b, preferred_element_type=F32)


def _dot_nt(a, b):
  return lax.dot_general(a, b, (((1,), (1,)), ((), ())), preferred_element_type=F32)


def _dot_tn(a, b):
  return lax.dot_general(a, b, (((0,), (0,)), ((), ())), preferred_element_type=F32)


def _split_bf16(x):
  hi = x.astype(BF16)
  lo = (x - hi.astype(F32)).astype(BF16)
  return hi, lo


def _rms_rows(x):
  return x * lax.rsqrt(jnp.mean(x * x, axis=-1, keepdims=True) + EPS)


def _mod_body(c_ref, w_ref, b_ref, o_ref):
  cv = c_ref[...]
  s = _silu(cv).astype(BF16)
  o_ref[...] = _dot(s, w_ref[...].astype(BF16)) + b_ref[...]


def _mod_call(cvec, w_ada, b_ada):
  n = w_ada.shape[1]
  bn = 1536
  return pl.pallas_call(
      _mod_body,
      grid=(n // bn,),
      in_specs=[
          pl.BlockSpec((8, D_MODEL), lambda j: (0, 0)),
          pl.BlockSpec((D_MODEL, bn), lambda j: (0, j)),
          pl.BlockSpec((1, bn), lambda j: (0, j)),
      ],
      out_specs=pl.BlockSpec((8, bn), lambda j: (0, j)),
      out_shape=jax.ShapeDtypeStruct((8, n), F32),
      compiler_params=pltpu.CompilerParams(
          dimension_semantics=("arbitrary",), vmem_limit_bytes=VMEM_LIMIT_V7X),
      name="adaln_mod",
  )(cvec, w_ada, b_ada)


N_BLOCKS = CHUNK // DIAG
N_UP = N_BLOCKS.bit_length() - 1


def _aligned(row):
  return row if isinstance(row, int) else pl.multiple_of(row, CHUNK)


def _query_blocks(level, dirn):
  return [blk for blk in range(N_BLOCKS) if ((blk >> (level - 1)) & 1) != dirn]


def _level_row(level, block, dirn):
  half = DIAG << (level - 1)
  start = (block >> level) * 2 * half
  return start + (half - 1 if dirn == 0 else half)


def _mix_body(*refs, layer, n_seq, seq_len, grid2d, has_state, emit_state):
  (x_ref, mod_ref, lbl_ref, n1_ref, win_ref, ng_ref, wa_ref, pw_ref, ps_ref, wb_ref,
   wo_ref) = refs[:11]
  rest = list(refs[11:])
  s0_ref = rest.pop(0) if has_state else None
  out_ref = rest.pop(0)
  st_ref = rest.pop(0) if emit_state else None
  (h_s, q_s, zf_s, v_s, og_s, zp_s, ga_s, gb_s, b_c, k_c, o_s, pm_s, st_s, tri_s, lev_s,
   band_s, icnt_s, tab_s) = rest[:18]
  cp_s = rest[18] if grid2d else None

  tt = n_seq * seq_len
  cps = seq_len // CHUNK

  ti = lax.broadcasted_iota(jnp.int32, (CHUNK, CHUNK), 0)
  si = lax.broadcasted_iota(jnp.int32, (CHUNK, CHUNK), 1)
  xr = ti ^ si
  lvl = jnp.zeros((CHUNK, CHUNK), jnp.int32)
  for j in range(1, N_UP + 1):
    lvl = jnp.where(xr >= (DIAG << (j - 1)), j, lvl)
  lev_s[0] = jnp.where(ti >= si, lvl, -1)
  lev_s[1] = jnp.where(ti <= si, lvl, -1)
  tri_s[0] = (ti >= si).astype(BF16)
  tri_s[1] = (ti <= si).astype(BF16)

  seg_shift = (GRID_W if grid2d else ROWS).bit_length() - 1
  tr = lax.broadcasted_iota(jnp.int32, (ROWS, ROWS), 0)
  sr = lax.broadcasted_iota(jnp.int32, (ROWS, ROWS), 1)
  same_seg = (tr >> seg_shift) == (sr >> seg_shift)
  for g, w in enumerate(POOL_WINDOWS):
    dlt = sr - tr
    band = (same_seg & (dlt >= -(w // 2)) & (dlt <= w // 2 - 1)).astype(F32)
    band_s[g] = band.astype(BF16)
    icnt_s[g] = 1.0 / jnp.sum(band, axis=1, keepdims=True)

  mod = mod_ref[...]
  sh1 = mod[:, 0:D_MODEL]
  a1 = n1_ref[...] * (1.0 + mod[:, D_MODEL:2 * D_MODEL])
  gate1 = mod[:, 2 * D_MODEL:3 * D_MODEL]
  lg = lbl_ref[...]
  e = jnp.exp(lg - jnp.max(lg, axis=0, keepdims=True))
  lb = jnp.sum(e[:layer + 1], axis=0) / jnp.sum(e, axis=0)

  for i in range(tt // ROWS):
    r = slice(i * ROWS, (i + 1) * ROWS)
    h_s[r, :] = (_rms_rows(x_ref[r, :]) * a1 + sh1).astype(BF16)
  hfull = h_s[...]

  def proj(c0, c1):
    return _dot(hfull, win_ref[:, c0:c1])

  q_s[...] = proj(0, D_A)
  zf_s[:, 0:D_A] = proj(D_A, 2 * D_A)
  zf_s[:, D_A:2 * D_A] = proj(2 * D_A, 3 * D_A)
  v_s[...] = proj(3 * D_A, 4 * D_A).astype(BF16)
  zo = proj(4 * D_A, 5 * D_A)
  og_s[...] = _silu(zo).astype(BF16)
  zp_s[...] = proj(5 * D_A, 5 * D_A + D_B)
  c0 = 5 * D_A + D_B
  for j in range(D_MODEL // 512):
    ga_s[:, j * 512:(j + 1) * 512] = _sigmoid(proj(c0 + j * 512, c0 + (j + 1) * 512)).astype(BF16)
  c0 += D_MODEL
  for j in range(D_MODEL // 512):
    gb_s[:, j * 512:(j + 1) * 512] = _sigmoid(proj(c0 + j * 512, c0 + (j + 1) * 512)).astype(BF16)

  half_c = 0.5 * (1.0 - lb)
  blk_i = lax.broadcasted_iota(jnp.int32, (N_BLOCKS, HEAD), 0)
  zero_tile = jnp.zeros((DIAG, HEAD), BF16)
  t_eb, t_et = 2 * N_UP, 2 * N_UP + 1

  blk_rows = lambda a, blk: a[blk * DIAG:(blk + 1) * DIAG]

  def stage_gates(base, dirn, slot):
    sd = 2 * slot + dirn
    c1 = half_c[dirn:dirn + 1, :]
    th = jnp.tanh(0.5 * zf_s[pl.ds(base, CHUNK), dirn * D_A:(dirn + 1) * D_A])
    ghi, glo = _split_bf16(jnp.log((1.0 - c1) + c1 * th))
    tri = tri_s[dirn]
    bsum = _dot(tri, ghi) + _dot(tri, glo)
    for hd in range(N_HEADS):
      b_c[sd, hd] = bsum[:, hd * HEAD:(hd + 1) * HEAD]
    k_c[sd] = c1 * (1.0 - th)

  def stage_factors(base, dirn, zero_state, slot, hd):
    sd = 2 * slot + dirn
    cs = slice(hd * HEAD, (hd + 1) * HEAD)
    crow = DIAG // 2 if dirn == 0 else DIAG // 2 - 1
    tot_row = CHUNK - 1 if dirn == 0 else 0
    brow = lambda r: b_c[sd, hd, r:r + 1, :]
    tot = brow(tot_row)
    beta = b_c[sd, hd, pl.ds(crow, N_BLOCKS, stride=DIAG), :]
    for l in range(1, N_UP + 1):
      n_par = N_BLOCKS >> l
      ref = brow(_level_row(l, (n_par - 1) << l, dirn))
      for p in range(n_par - 2, -1, -1):
        ref = jnp.where(blk_i < ((p + 1) << l), brow(_level_row(l, p << l, dirn)), ref)
      d = beta - ref
      tab_s[sd, hd, l - 1] = jnp.exp(jnp.minimum(d, 0.0))
      tab_s[sd, hd, N_UP + l - 1] = jnp.exp(jnp.minimum(-d, 0.0))
    tab_s[sd, hd, t_eb] = jnp.exp(beta)
    tab_s[sd, hd, t_et] = jnp.exp(tot - beta)
    trow = lambda t, blk: jnp.broadcast_to(tab_s[sd, hd, t, blk:blk + 1, :], (DIAG, HEAD))

    beta_b = jnp.concatenate(
        [jnp.broadcast_to(brow(blk * DIAG + crow), (DIAG, HEAD)) for blk in range(N_BLOCKS)],
        axis=0)
    de = b_c[sd, hd] - beta_b
    a0 = q_s[pl.ds(base, CHUNK), cs] * jnp.exp(jnp.minimum(de, EXP_CLAMP))
    b0 = k_c[sd, :, cs] * jnp.exp(jnp.minimum(-de, EXP_CLAMP))
    ops = dict(base=base, dirn=dirn, hd=hd, zero_state=zero_state, tot=tot,
               a0=a0.astype(BF16), b0=b0.astype(BF16), lhs={}, rhs={})
    for l in range(1, N_UP + 1):
      qb = _query_blocks(l, dirn)
      ops["lhs"][l] = jnp.concatenate(
          [(blk_rows(a0, blk) * trow(l - 1, blk)).astype(BF16) for blk in qb], axis=0)
      ops["rhs"][l] = jnp.concatenate(
          [zero_tile if blk in qb else (blk_rows(b0, blk) * trow(N_UP + l - 1, blk)).astype(BF16)
           for blk in range(N_BLOCKS)], axis=0)
    ops["kt"] = jnp.concatenate(
        [(blk_rows(b0, blk) * trow(t_et, blk)).astype(BF16) for blk in range(N_BLOCKS)], axis=0)
    if not zero_state:
      ops["qt"] = jnp.concatenate(
          [(blk_rows(a0, blk) * trow(t_eb, blk)).astype(BF16) for blk in range(N_BLOCKS)], axis=0)
    return ops

  def stage_scores(ops):
    dirn = ops["dirn"]
    p_same = _dot_nt(ops["a0"], ops["b0"])
    p_lev = {l: _dot_nt(ops["lhs"][l], ops["rhs"][l]) for l in range(1, N_UP + 1)}
    sc_rows = []
    for blk in range(N_BLOCKS):
      lev = lev_s[dirn, blk * DIAG:(blk + 1) * DIAG, :]
      r = jnp.where(lev == 0, blk_rows(p_same, blk), 0.0)
      for l in range(1, N_UP + 1):
        qb = _query_blocks(l, dirn)
        if blk in qb:
          r = jnp.where(lev == l, blk_rows(p_lev[l], qb.index(blk)), r)
      sc_rows.append(r.astype(BF16))
    ops["sc"] = jnp.concatenate(sc_rows, axis=0)

  def stage_out(ops):
    dirn, hd = ops["dirn"], ops["hd"]
    rows = pl.ds(ops["base"], CHUNK)
    cs = slice(hd * HEAD, (hd + 1) * HEAD)
    vh = v_s[rows, cs]
    o = _dot(ops["sc"], vh)
    if ops["zero_state"]:
      st_s[dirn, hd] = _dot_tn(vh, ops["kt"])
    else:
      st = st_s[dirn, hd]
      o = o + _dot_nt(ops["qt"], st.astype(BF16))
      st_s[dirn, hd] = st * jnp.exp(ops["tot"]) + _dot_tn(vh, ops["kt"])
    o_s[dirn, rows, cs] = o

  def scan_group(chunks):
    for base, dirn, _, slot in chunks:
      stage_gates(base, dirn, slot)
    group = [stage_factors(base, dirn, zero_state, slot, hd)
             for base, dirn, zero_state, slot in chunks for hd in range(N_HEADS)]
    for ops in group:
      stage_scores(ops)
    for ops in group:
      stage_out(ops)

  def chunk_pair(seq_base, ci, zero_state, slot):
    return [(_aligned(seq_base + ci * CHUNK), 0, zero_state, slot),
            (_aligned(seq_base + (cps - 1 - ci) * CHUNK), 1, zero_state, slot)]

  def scan_sequence(seq_base):
    if has_state:
      for dirn in range(2):
        for hd in range(N_HEADS):
          st_s[dirn, hd] = s0_ref[dirn, hd].T

      def body(cj, carry):
        scan_group([c for slot in range(CHUNK_SLOTS)
                    for c in chunk_pair(seq_base, cj * CHUNK_SLOTS + slot, False, slot)])
        return carry

      lax.fori_loop(0, cps // CHUNK_SLOTS, body, 0)
    else:
      assert cps <= CHUNK_SLOTS
      scan_group([c for ci in range(cps) for c in chunk_pair(seq_base, ci, ci == 0, ci)])

  if n_seq == 1:
    scan_sequence(0)
    if emit_state:
      for dirn in range(2):
        for hd in range(N_HEADS):
          st_ref[0, dirn, hd] = st_s[dirn, hd].T
  else:
    def seq_body(sq, carry):
      scan_sequence(sq * seq_len)
      if emit_state:
        for dirn in range(2):
          for hd in range(N_HEADS):
            st_ref[sq, dirn, hd] = st_s[dirn, hd].T
      return carry

    lax.fori_loop(0, n_seq, seq_body, 0)

  def band_pool(r, g):
    gs = slice(g * GROUP, (g + 1) * GROUP)
    zhi, zlo = _split_bf16(zp_s[r, gs])
    return (_dot(band_s[g], zhi) + _dot(band_s[g], zlo)) * icnt_s[g]

  if not grid2d:
    for i in range(tt // ROWS):
      r = slice(i * ROWS, (i + 1) * ROWS)
      for g in range(N_GROUPS):
        gs = slice(g * GROUP, (g + 1) * GROUP)
        pm_s[r, gs] = (band_pool(r, g) - zp_s[r, gs]).astype(BF16)
  else:
    n_rows = seq_len // GRID_W
    for i in range(tt // ROWS):
      r = slice(i * ROWS, (i + 1) * ROWS)
      for g in range(N_GROUPS):
        cp_s[r, g * GROUP:(g + 1) * GROUP] = band_pool(r, g)
    for g, w in enumerate(POOL_WINDOWS):
      gs = slice(g * GROUP, (g + 1) * GROUP)
      acc = None
      lo_prev = hi_prev = None
      for rr in range(n_rows):
        lo = max(rr - w // 2, 0)
        hi = min(rr + w // 2 - 1, n_rows - 1)
        if rr == 0:
          acc = cp_s[0:GRID_W, gs]
          for k in range(1, hi + 1):
            acc = acc + cp_s[k * GRID_W:(k + 1) * GRID_W, gs]
        else:
          if hi > hi_prev:
            acc = acc + cp_s[hi * GRID_W:(hi + 1) * GRID_W, gs]
          if lo > lo_prev:
            acc = acc - cp_s[lo_prev * GRID_W:(lo_prev + 1) * GRID_W, gs]
        lo_prev, hi_prev = lo, hi
        rs = slice(rr * GRID_W, (rr + 1) * GRID_W)
        pm_s[rs, gs] = (acc * (1.0 / (hi - lo + 1)) - zp_s[rs, gs]).astype(BF16)

  ng = ng_ref[...]
  psc = ps_ref[...]
  for i in range(tt // ROWS):
    r = slice(i * ROWS, (i + 1) * ROWS)
    parts = []
    for hd in range(N_HEADS):
      cs = slice(hd * HEAD, (hd + 1) * HEAD)
      parts.append(_rms_rows(o_s[0, r, cs] + o_s[1, r, cs]) * ng[:, cs])
    oa = (jnp.concatenate(parts, axis=1) * og_s[r, :].astype(F32)).astype(BF16)
    ya = _dot(oa, wa_ref[...])
    obp = []
    for g in range(N_GROUPS):
      gs = slice(g * GROUP, (g + 1) * GROUP)
      obp.append(_dot(pm_s[r, gs], pw_ref[g]) * psc[:, gs])
    ob = jnp.concatenate(obp, axis=1).astype(BF16)
    yb = _dot(ob, wb_ref[...])
    merged = (ga_s[r, :].astype(F32) * ya + gb_s[r, :].astype(F32) * yb).astype(BF16)
    out_ref[r, :] = x_ref[r, :] + gate1 * _dot(merged, wo_ref[...])


def _const_spec(shape):
  nd = len(shape)
  return pl.BlockSpec(shape, lambda i, _nd=nd: (0,) * _nd, pipeline_mode=pl.Buffered(1))


def _mix_call(x2d, mod3, mod_row, lbl, n1, win, ng, wa, pw, ps, wb, wo, s0, *, layer, n_seq,
              seq_len, grid2d, emit_state):
  tokens = x2d.shape[0]
  tt = n_seq * seq_len
  assert tokens % tt == 0 and seq_len % (CHUNK * CHUNK_SLOTS) == 0 and tt % ROWS == 0
  steps = tokens // tt
  has_state = s0 is not None
  x_mode = pl.Buffered(1) if steps <= 2 else None
  in_specs = [
      pl.BlockSpec((tt, D_MODEL), lambda i: (i, 0), pipeline_mode=x_mode),
      pl.BlockSpec((None, 1, 6 * D_MODEL), lambda i: (mod_row(i), 0, 0)),
      _const_spec(lbl.shape), _const_spec(n1.shape), _const_spec(win.shape), _const_spec(ng.shape),
      _const_spec(wa.shape), _const_spec(pw.shape), _const_spec(ps.shape), _const_spec(wb.shape),
      _const_spec(wo.shape),
  ]
  args = [x2d, mod3, lbl, n1, win, ng, wa, pw, ps, wb, wo]
  if has_state:
    in_specs.append(pl.BlockSpec((None, None, 2, N_HEADS, HEAD, HEAD),
                                 lambda i: (i, layer, 0, 0, 0, 0)))
    args.append(s0)
  out_specs = [pl.BlockSpec((tt, D_MODEL), lambda i: (i, 0))]
  out_shape = [jax.ShapeDtypeStruct((tokens, D_MODEL), F32)]
  if emit_state:
    out_specs.append(pl.BlockSpec((n_seq, 2, N_HEADS, HEAD, HEAD), lambda i: (i, 0, 0, 0, 0)))
    out_shape.append(jax.ShapeDtypeStruct((tokens // seq_len, 2, N_HEADS, HEAD, HEAD), F32))
  scratch = [
      pltpu.VMEM((tt, D_MODEL), BF16),
      pltpu.VMEM((tt, D_A), F32),
      pltpu.VMEM((tt, 2 * D_A), F32),
      pltpu.VMEM((tt, D_A), BF16),
      pltpu.VMEM((tt, D_A), BF16),
      pltpu.VMEM((tt, D_B), F32),
      pltpu.VMEM((tt, D_MODEL), BF16),
      pltpu.VMEM((tt, D_MODEL), BF16),
      pltpu.VMEM((2 * CHUNK_SLOTS, N_HEADS, CHUNK, HEAD), F32),
      pltpu.VMEM((2 * CHUNK_SLOTS, CHUNK, D_A), F32),
      pltpu.VMEM((2, tt, D_A), F32),
      pltpu.VMEM((tt, D_B), BF16),
      pltpu.VMEM((2, N_HEADS, HEAD, HEAD), F32),
      pltpu.VMEM((2, CHUNK, CHUNK), BF16),
      pltpu.VMEM((2, CHUNK, CHUNK), jnp.int32),
      pltpu.VMEM((N_GROUPS, ROWS, ROWS), BF16),
      pltpu.VMEM((N_GROUPS, ROWS, 1), F32),
      pltpu.VMEM((2 * CHUNK_SLOTS, N_HEADS, 2 * N_UP + 2, N_BLOCKS, HEAD), F32),
  ]
  if grid2d:
    scratch.append(pltpu.VMEM((tt, D_B), F32))
  body = functools.partial(_mix_body, layer=layer, n_seq=n_seq, seq_len=seq_len, grid2d=grid2d,
                           has_state=has_state, emit_state=emit_state)
  return pl.pallas_call(
      body,
      grid=(steps,),
      in_specs=in_specs,
      out_specs=out_specs,
      out_shape=out_shape,
      scratch_shapes=scratch,
      compiler_params=pltpu.CompilerParams(
          dimension_semantics=("arbitrary",), vmem_limit_bytes=VMEM_LIMIT_V7X),
      name="mix_grid2d" if grid2d else "mix_seq1d",
  )(*args)


FFN_COLS = 256


WI_CHUNK = 512
WO_CHUNK = 256


def _stream_cast(w_hbm, w_s, stage, sems, chunks):
  def copy(k):
    return pltpu.make_async_copy(w_hbm.at[chunks[k]], stage.at[k % 2], sems.at[k % 2])

  copy(0).start()
  for k in range(len(chunks)):
    if k + 1 < len(chunks):
      copy(k + 1).start()
    copy(k).wait()
    w_s[chunks[k]] = stage[k % 2].astype(BF16)


def _ffn_body(xc_ref, xl_ref, mod_ref, n2_ref, wi_hbm, wo_hbm, fg_ref, oc_ref, ol_ref, h_s, hid_s,
              y_s, wi_s, wo_s, sti_s, sto_s, sems, *, ctx_steps):
  tt = xc_ref.shape[0]
  step = pl.program_id(0)
  is_ctx = step < ctx_steps

  @pl.when(step == 0)
  def _():
    _stream_cast(wi_hbm, wi_s, sti_s, sems.at[0],
                 [(slice(None), pl.ds(c, WI_CHUNK)) for c in range(0, 2 * D_FF, WI_CHUNK)])
    _stream_cast(wo_hbm, wo_s, sto_s, sems.at[1],
                 [(pl.ds(r, WO_CHUNK), slice(None)) for r in range(0, D_FF, WO_CHUNK)])

  mod = mod_ref[...]
  sh2 = mod[:, 3 * D_MODEL:4 * D_MODEL]
  a2 = n2_ref[...] * (1.0 + mod[:, 4 * D_MODEL:5 * D_MODEL])
  gate2 = mod[:, 5 * D_MODEL:6 * D_MODEL]

  def norm_in(x_ref):
    for i in range(tt // ROWS):
      r = slice(i * ROWS, (i + 1) * ROWS)
      h_s[r, :] = (_rms_rows(x_ref[r, :]) * a2 + sh2).astype(BF16)

  pl.when(is_ctx)(lambda: norm_in(xc_ref))
  pl.when(jnp.logical_not(is_ctx))(lambda: norm_in(xl_ref))
  h = h_s[...]
  for j in range(D_FF // FFN_COLS):
    cg = slice(j * FFN_COLS, (j + 1) * FFN_COLS)
    cu = slice(D_FF + j * FFN_COLS, D_FF + (j + 1) * FFN_COLS)
    gp = _dot(h, wi_s[:, cg])
    up = _dot(h, wi_s[:, cu])
    hid_s[:, cg] = (_silu(gp) * up).astype(BF16)
  y_s[...] = _dot(hid_s[...], wo_s[...])
  fg = fg_ref[...]

  def finish(x_ref, o_ref):
    for i in range(tt // ROWS):
      r = slice(i * ROWS, (i + 1) * ROWS)
      o_ref[r, :] = _rms_rows(x_ref[r, :] + gate2 * y_s[r, :]) * fg

  pl.when(is_ctx)(lambda: finish(xc_ref, oc_ref))
  pl.when(jnp.logical_not(is_ctx))(lambda: finish(xl_ref, ol_ref))


def _ffn_call(x_ctx, x_lat, mod3, n2, wi, wo, fg, *, tt, lat_seq_len):
  ctx_tokens, lat_tokens = x_ctx.shape[0], x_lat.shape[0]
  assert ctx_tokens % tt == 0 and lat_tokens % tt == 0 and lat_seq_len % tt == 0 and tt % ROWS == 0
  assert D_FF % FFN_COLS == 0 and (2 * D_FF) % WI_CHUNK == 0 and D_FF % WO_CHUNK == 0
  nc = ctx_tokens // tt
  steps_per_seq = lat_seq_len // tt
  mod_row = lambda i: jnp.where(i < nc, 0, 1 + (i - nc) // steps_per_seq)
  ctx_blk = lambda i: (jnp.minimum(i, nc - 1), 0)
  lat_blk = lambda i: (jnp.maximum(i - nc, 0), 0)
  body = functools.partial(_ffn_body, ctx_steps=nc)
  return pl.pallas_call(
      body,
      grid=((ctx_tokens + lat_tokens) // tt,),
      in_specs=[
          pl.BlockSpec((tt, D_MODEL), ctx_blk),
          pl.BlockSpec((tt, D_MODEL), lat_blk),
          pl.BlockSpec((None, 1, 6 * D_MODEL), lambda i: (mod_row(i), 0, 0)),
          _const_spec(n2.shape),
          pl.BlockSpec(memory_space=pl.ANY),
          pl.BlockSpec(memory_space=pl.ANY),
          _const_spec(fg.shape),
      ],
      out_specs=[
          pl.BlockSpec((tt, D_MODEL), ctx_blk),
          pl.BlockSpec((tt, D_MODEL), lat_blk),
      ],
      out_shape=[jax.ShapeDtypeStruct((ctx_tokens, D_MODEL), F32),
                 jax.ShapeDtypeStruct((lat_tokens, D_MODEL), F32)],
      scratch_shapes=[
          pltpu.VMEM((tt, D_MODEL), BF16),
          pltpu.VMEM((tt, D_FF), BF16),
          pltpu.VMEM((tt, D_MODEL), F32),
          pltpu.VMEM((D_MODEL, 2 * D_FF), BF16),
          pltpu.VMEM((D_FF, D_MODEL), BF16),
          pltpu.VMEM((2, D_MODEL, WI_CHUNK), F32),
          pltpu.VMEM((2, WO_CHUNK, D_MODEL), F32),
          pltpu.SemaphoreType.DMA((2, 2)),
      ],
      compiler_params=pltpu.CompilerParams(
          dimension_semantics=("arbitrary",), vmem_limit_bytes=VMEM_LIMIT_V7X),
      name="ffn_final",
  )(x_ctx, x_lat, mod3, n2, wi, wo, fg)


def kernel(x_prompt, x_sample, state_hgrn, c, c_ctx, w_ada, b_ada, norm1_g, w_in, hgrn_lb_logits,
           hgrn_norm_g, w_branch_a, pool_w, pool_scale, w_branch_b, w_out, norm2_g, w_ffn_in,
           w_ffn_out, final_g):
  bp, sp, _ = x_prompt.shape
  bs, ss, _ = x_sample.shape
  depth = w_in.shape[0]
  assert bs + 1 <= 8 and ss % GRID_W == 0
  xp = x_prompt.reshape(bp * sp, D_MODEL)
  xs = x_sample.reshape(bs * ss, D_MODEL)
  cvec = jnp.concatenate([c_ctx[None, :], c, jnp.zeros((8 - 1 - bs, D_MODEL), F32)], axis=0)
  row2 = lambda a: a.reshape(1, -1)
  ffn_tt = 512
  ctx_states = []
  for l in range(depth):
    mod3 = _mod_call(cvec, w_ada[l], row2(b_ada[l]))[:1 + bs, None, :]
    mix_w = (hgrn_lb_logits, row2(norm1_g[l]), w_in[l].astype(BF16), row2(hgrn_norm_g[l]),
             w_branch_a[l].astype(BF16), pool_w[l].astype(BF16), row2(pool_scale[l]),
             w_branch_b[l].astype(BF16), w_out[l].astype(BF16))
    assert l == depth - 1, "the FFN call also applies the final norm"
    xp, st = _mix_call(xp, mod3, lambda i: 0, *mix_w, None, layer=l, n_seq=2, seq_len=sp,
                       grid2d=False, emit_state=True)
    ctx_states.append(st)
    (xs,) = _mix_call(xs, mod3, lambda i: 1 + i, *mix_w, state_hgrn, layer=l, n_seq=1,
                      seq_len=ss, grid2d=True, emit_state=False)
    xp, xs = _ffn_call(xp, xs, mod3, row2(norm2_g[l]), w_ffn_in[l], w_ffn_out[l], row2(final_g),
                       tt=ffn_tt, lat_seq_len=ss)
  new_state = jnp.stack(ctx_states, axis=1).astype(x_prompt.dtype)
  return (xp.reshape(bp, sp, D_MODEL), xs.reshape(bs, ss, D_MODEL), new_state)
```

```python
import functools

import jax
import jax.numpy as jnp
from jax import lax
from jax.experimental import pallas as pl
from jax.experimental.pallas import tpu as pltpu

F32 = jnp.float32
BF16 = jnp.bfloat16

D_MODEL = 1024
D_A = 512
N_HEADS = 4
HEAD = D_A // N_HEADS
D_B = 512
N_GROUPS = 4
GROUP = D_B // N_GROUPS
POOL_WINDOWS = (2, 4, 8, 16)
GRID_W = 64
D_FF = 2816
IN_COLS = 5 * D_A + D_B + 2 * D_MODEL
EPS = 1e-6

CHUNK = 128
DIAG = 16
CHUNK_SLOTS = 2
ROWS = 256
EXP_CLAMP = 80.0
VMEM_LIMIT_V7X = 58 * 1024 * 1024


def _sigmoid(x):
  return 0.5 + 0.5 * jnp.tanh(0.5 * x)


def _silu(x):
  h = 0.5 * x
  return h + h * jnp.tanh(h)


def _dot(a, b):
  return jnp.dot(a, b, preferred_element_type=F32)


def _dot_nt(a, b):
  return lax.dot_general(a, b, (((1,), (1,)), ((), ())), preferred_element_type=F32)


def _dot_tn(a, b):
  return lax.dot_general(a, b, (((0,), (0,)), ((), ())), preferred_element_type=F32)


def _split_bf16(x):
  hi = x.astype(BF16)
  lo = (x - hi.astype(F32)).astype(BF16)
  return hi, lo


def _rms_rows(x):
  return x * lax.rsqrt(jnp.mean(x * x, axis=-1, keepdims=True) + EPS)


def _mod_body(c_ref, w_ref, b_ref, o_ref):
  cv = c_ref[...]
  s = _silu(cv).astype(BF16)
  o_ref[...] = _dot(s, w_ref[...].astype(BF16)) + b_ref[...]


def _mod_call(cvec, w_ada, b_ada):
  n = w_ada.shape[1]
  bn = 1536
  return pl.pallas_call(
      _mod_body,
      grid=(n // bn,),
      in_specs=[
          pl.BlockSpec((8, D_MODEL), lambda j: (0, 0)),
          pl.BlockSpec((D_MODEL, bn), lambda j: (0, j)),
          pl.BlockSpec((1, bn), lambda j: (0, j)),
      ],
      out_specs=pl.BlockSpec((8, bn), lambda j: (0, j)),
      out_shape=jax.ShapeDtypeStruct((8, n), F32),
      compiler_params=pltpu.CompilerParams(
          dimension_semantics=("arbitrary",), vmem_limit_bytes=VMEM_LIMIT_V7X),
      name="adaln_mod",
  )(cvec, w_ada, b_ada)


N_BLOCKS = CHUNK // DIAG
N_UP = N_BLOCKS.bit_length() - 1


def _aligned(row):
  return row if isinstance(row, int) else pl.multiple_of(row, CHUNK)


def _query_blocks(level, dirn):
  return [blk for blk in range(N_BLOCKS) if ((blk >> (level - 1)) & 1) != dirn]


def _level_row(level, block, dirn):
  half = DIAG << (level - 1)
  start = (block >> level) * 2 * half
  return start + (half - 1 if dirn == 0 else half)


def _mix_body(*refs, layer, n_seq, seq_len, grid2d, has_state, emit_state):
  (x_ref, mod_ref, lbl_ref, n1_ref, win_ref, ng_ref, wa_ref, pw_ref, ps_ref, wb_ref,
   wo_ref) = refs[:11]
  rest = list(refs[11:])
  s0_ref = rest.pop(0) if has_state else None
  out_ref = rest.pop(0)
  st_ref = rest.pop(0) if emit_state else None
  (h_s, q_s, zf_s, v_s, og_s, zp_s, ga_s, gb_s, b_c, k_c, o_s, pm_s, st_s, tri_s, lev_s,
   band_s, icnt_s, tab_s) = rest[:18]
  cp_s = rest[18] if grid2d else None

  tt = n_seq * seq_len
  cps = seq_len // CHUNK

  ti = lax.broadcasted_iota(jnp.int32, (CHUNK, CHUNK), 0)
  si = lax.broadcasted_iota(jnp.int32, (CHUNK, CHUNK), 1)
  xr = ti ^ si
  lvl = jnp.zeros((CHUNK, CHUNK), jnp.int32)
  for j in range(1, N_UP + 1):
    lvl = jnp.where(xr >= (DIAG << (j - 1)), j, lvl)
  lev_s[0] = jnp.where(ti >= si, lvl, -1)
  lev_s[1] = jnp.where(ti <= si, lvl, -1)
  tri_s[0] = (ti >= si).astype(BF16)
  tri_s[1] = (ti <= si).astype(BF16)

  seg_shift = (GRID_W if grid2d else ROWS).bit_length() - 1
  tr = lax.broadcasted_iota(jnp.int32, (ROWS, ROWS), 0)
  sr = lax.broadcasted_iota(jnp.int32, (ROWS, ROWS), 1)
  same_seg = (tr >> seg_shift) == (sr >> seg_shift)
  for g, w in enumerate(POOL_WINDOWS):
    dlt = sr - tr
    band = (same_seg & (dlt >= -(w // 2)) & (dlt <= w // 2 - 1)).astype(F32)
    band_s[g] = band.astype(BF16)
    icnt_s[g] = 1.0 / jnp.sum(band, axis=1, keepdims=True)

  mod = mod_ref[...]
  sh1 = mod[:, 0:D_MODEL]
  a1 = n1_ref[...] * (1.0 + mod[:, D_MODEL:2 * D_MODEL])
  gate1 = mod[:, 2 * D_MODEL:3 * D_MODEL]
  lg = lbl_ref[...]
  e = jnp.exp(lg - jnp.max(lg, axis=0, keepdims=True))
  lb = jnp.sum(e[:layer + 1], axis=0) / jnp.sum(e, axis=0)

  def run(units):
    for u in units:
      u()

  def interleave(main, side):
    out, taken = [], 0
    for k, u in enumerate(main):
      out.append(u)
      want = ((k + 1) * len(side)) // len(main)
      out.extend(side[taken:want])
      taken = want
    return out

  def phase1_units(r0, n):
    rr = slice(r0, r0 + n)

    def norm():
      for i in range(n // ROWS):
        r = slice(r0 + i * ROWS, r0 + (i + 1) * ROWS)
        h_s[r, :] = (_rms_rows(x_ref[r, :]) * a1 + sh1).astype(BF16)

    def proj(c0, c1):
      return _dot(h_s[rr, :], win_ref[:, c0:c1])

    def put(dst, cols, c0, act, dtype):
      def unit():
        z = proj(c0, c0 + (cols.stop - cols.start))
        dst[rr, cols] = (z if act is None else act(z)).astype(dtype)
      return unit

    half = slice(0, D_A), slice(D_A, 2 * D_A)
    units = [norm,
             put(q_s, half[0], 0, None, F32),
             put(zf_s, half[0], D_A, None, F32),
             put(zf_s, half[1], 2 * D_A, None, F32),
             put(v_s, half[0], 3 * D_A, None, BF16),
             put(og_s, half[0], 4 * D_A, _silu, BF16),
             put(zp_s, half[0], 5 * D_A, None, F32)]
    c0 = 5 * D_A + D_B
    for dst in (ga_s, gb_s):
      for j in range(D_MODEL // 512):
        units.append(put(dst, slice(j * 512, (j + 1) * 512), c0, _sigmoid, BF16))
        c0 += 512
    return units

  half_c = 0.5 * (1.0 - lb)
  blk_i = lax.broadcasted_iota(jnp.int32, (N_BLOCKS, HEAD), 0)
  zero_tile = jnp.zeros((DIAG, HEAD), BF16)
  t_eb, t_et = 2 * N_UP, 2 * N_UP + 1

  blk_rows = lambda a, blk: a[blk * DIAG:(blk + 1) * DIAG]

  def stage_gates(base, dirn, slot):
    sd = 2 * slot + dirn
    c1 = half_c[dirn:dirn + 1, :]
    th = jnp.tanh(0.5 * zf_s[pl.ds(base, CHUNK), dirn * D_A:(dirn + 1) * D_A])
    ghi, glo = _split_bf16(jnp.log((1.0 - c1) + c1 * th))
    tri = tri_s[dirn]
    bsum = _dot(tri, ghi) + _dot(tri, glo)
    for hd in range(N_HEADS):
      b_c[sd, hd] = bsum[:, hd * HEAD:(hd + 1) * HEAD]
    k_c[sd] = c1 * (1.0 - th)

  def stage_factors(base, dirn, zero_state, slot, hd):
    sd = 2 * slot + dirn
    cs = slice(hd * HEAD, (hd + 1) * HEAD)
    crow = DIAG // 2 if dirn == 0 else DIAG // 2 - 1
    tot_row = CHUNK - 1 if dirn == 0 else 0
    brow = lambda r: b_c[sd, hd, r:r + 1, :]
    tot = brow(tot_row)
    beta = b_c[sd, hd, pl.ds(crow, N_BLOCKS, stride=DIAG), :]
    for l in range(1, N_UP + 1):
      n_par = N_BLOCKS >> l
      ref = brow(_level_row(l, (n_par - 1) << l, dirn))
      for p in range(n_par - 2, -1, -1):
        ref = jnp.where(blk_i < ((p + 1) << l), brow(_level_row(l, p << l, dirn)), ref)
      d = beta - ref
      tab_s[sd, hd, l - 1] = jnp.exp(jnp.minimum(d, 0.0))
      tab_s[sd, hd, N_UP + l - 1] = jnp.exp(jnp.minimum(-d, 0.0))
    tab_s[sd, hd, t_eb] = jnp.exp(beta)
    tab_s[sd, hd, t_et] = jnp.exp(tot - beta)
    trow = lambda t, blk: jnp.broadcast_to(tab_s[sd, hd, t, blk:blk + 1, :], (DIAG, HEAD))

    beta_b = jnp.concatenate(
        [jnp.broadcast_to(brow(blk * DIAG + crow), (DIAG, HEAD)) for blk in range(N_BLOCKS)],
        axis=0)
    de = b_c[sd, hd] - beta_b
    a0 = q_s[pl.ds(base, CHUNK), cs] * jnp.exp(jnp.minimum(de, EXP_CLAMP))
    b0 = k_c[sd, :, cs] * jnp.exp(jnp.minimum(-de, EXP_CLAMP))
    ops = dict(base=base, dirn=dirn, hd=hd, zero_state=zero_state, tot=tot,
               a0=a0.astype(BF16), b0=b0.astype(BF16), lhs={}, rhs={})
    for l in range(1, N_UP + 1):
      qb = _query_blocks(l, dirn)
      ops["lhs"][l] = jnp.concatenate(
          [(blk_rows(a0, blk) * trow(l - 1, blk)).astype(BF16) for blk in qb], axis=0)
      ops["rhs"][l] = jnp.concatenate(
          [zero_tile if blk in qb else (blk_rows(b0, blk) * trow(N_UP + l - 1, blk)).astype(BF16)
           for blk in range(N_BLOCKS)], axis=0)
    ops["kt"] = jnp.concatenate(
        [(blk_rows(b0, blk) * trow(t_et, blk)).astype(BF16) for blk in range(N_BLOCKS)], axis=0)
    if not zero_state:
      ops["qt"] = jnp.concatenate(
          [(blk_rows(a0, blk) * trow(t_eb, blk)).astype(BF16) for blk in range(N_BLOCKS)], axis=0)
    return ops

  def stage_scores(ops):
    dirn = ops["dirn"]
    p_same = _dot_nt(ops["a0"], ops["b0"])
    p_lev = {l: _dot_nt(ops["lhs"][l], ops["rhs"][l]) for l in range(1, N_UP + 1)}
    sc_rows = []
    for blk in range(N_BLOCKS):
      lev = lev_s[dirn, blk * DIAG:(blk + 1) * DIAG, :]
      r = jnp.where(lev == 0, blk_rows(p_same, blk), 0.0)
      for l in range(1, N_UP + 1):
        qb = _query_blocks(l, dirn)
        if blk in qb:
          r = jnp.where(lev == l, blk_rows(p_lev[l], qb.index(blk)), r)
      sc_rows.append(r.astype(BF16))
    ops["sc"] = jnp.concatenate(sc_rows, axis=0)

  def stage_out(ops):
    dirn, hd = ops["dirn"], ops["hd"]
    rows = pl.ds(ops["base"], CHUNK)
    cs = slice(hd * HEAD, (hd + 1) * HEAD)
    vh = v_s[rows, cs]
    o = _dot(ops["sc"], vh)
    if ops["zero_state"]:
      st_s[dirn, hd] = _dot_tn(vh, ops["kt"])
    else:
      st = st_s[dirn, hd]
      o = o + _dot_nt(ops["qt"], st.astype(BF16))
      st_s[dirn, hd] = st * jnp.exp(ops["tot"]) + _dot_tn(vh, ops["kt"])
    o_s[dirn, rows, cs] = o

  def scan_units(chunks):
    group = []
    units = [functools.partial(stage_gates, base, dirn, slot) for base, dirn, _, slot in chunks]
    for base, dirn, zero_state, slot in chunks:
      for hd in range(N_HEADS):
        units.append(lambda a=(base, dirn, zero_state, slot, hd): group.append(stage_factors(*a)))
    n = len(chunks) * N_HEADS
    units += [lambda k=k: stage_scores(group[k]) for k in range(n)]
    units += [lambda k=k: stage_out(group[k]) for k in range(n)]
    return units

  def chunk_pair(seq_base, ci, zero_state, slot):
    return [(_aligned(seq_base + ci * CHUNK), 0, zero_state, slot),
            (_aligned(seq_base + (cps - 1 - ci) * CHUNK), 1, zero_state, slot)]

  def emit_states(sq):
    for dirn in range(2):
      for hd in range(N_HEADS):
        st_ref[sq, dirn, hd] = st_s[dirn, hd].T

  def band_pool(r, g):
    gs = slice(g * GROUP, (g + 1) * GROUP)
    zhi, zlo = _split_bf16(zp_s[r, gs])
    return (_dot(band_s[g], zhi) + _dot(band_s[g], zlo)) * icnt_s[g]

  def pool1d_units(r0):
    r = slice(r0, r0 + ROWS)

    def unit(g):
      gs = slice(g * GROUP, (g + 1) * GROUP)
      pm_s[r, gs] = (band_pool(r, g) - zp_s[r, gs]).astype(BF16)
    return [functools.partial(unit, g) for g in range(N_GROUPS)]

  def pool2d():
    n_rows = seq_len // GRID_W
    for i in range(tt // ROWS):
      r = slice(i * ROWS, (i + 1) * ROWS)
      for g in range(N_GROUPS):
        cp_s[r, g * GROUP:(g + 1) * GROUP] = band_pool(r, g)
    for g, w in enumerate(POOL_WINDOWS):
      gs = slice(g * GROUP, (g + 1) * GROUP)
      acc = None
      lo_prev = hi_prev = None
      for rr in range(n_rows):
        lo = max(rr - w // 2, 0)
        hi = min(rr + w // 2 - 1, n_rows - 1)
        if rr == 0:
          acc = cp_s[0:GRID_W, gs]
          for k in range(1, hi + 1):
            acc = acc + cp_s[k * GRID_W:(k + 1) * GRID_W, gs]
        else:
          if hi > hi_prev:
            acc = acc + cp_s[hi * GRID_W:(hi + 1) * GRID_W, gs]
          if lo > lo_prev:
            acc = acc - cp_s[lo_prev * GRID_W:(lo_prev + 1) * GRID_W, gs]
        lo_prev, hi_prev = lo, hi
        rs = slice(rr * GRID_W, (rr + 1) * GRID_W)
        pm_s[rs, gs] = (acc * (1.0 / (hi - lo + 1)) - zp_s[rs, gs]).astype(BF16)

  ng = ng_ref[...]
  psc = ps_ref[...]

  def phase4_units(r0):
    r = slice(r0, r0 + ROWS)
    vals = {}

    def branch_a():
      parts = []
      for hd in range(N_HEADS):
        cs = slice(hd * HEAD, (hd + 1) * HEAD)
        parts.append(_rms_rows(o_s[0, r, cs] + o_s[1, r, cs]) * ng[:, cs])
      oa = (jnp.concatenate(parts, axis=1) * og_s[r, :].astype(F32)).astype(BF16)
      vals["ya"] = _dot(oa, wa_ref[...])

    def branch_b():
      obp = []
      for g in range(N_GROUPS):
        gs = slice(g * GROUP, (g + 1) * GROUP)
        obp.append(_dot(pm_s[r, gs], pw_ref[g]) * psc[:, gs])
      ob = jnp.concatenate(obp, axis=1).astype(BF16)
      vals["yb"] = _dot(ob, wb_ref[...])

    def merge_out():
      merged = (ga_s[r, :].astype(F32) * vals["ya"]
                + gb_s[r, :].astype(F32) * vals["yb"]).astype(BF16)
      out_ref[r, :] = x_ref[r, :] + gate1 * _dot(merged, wo_ref[...])

    return [branch_a, branch_b, merge_out]

  row_blocks = lambda r0, n: range(r0, r0 + n, ROWS)

  if has_state:
    assert n_seq == 1 and grid2d and not emit_state
    run(phase1_units(0, tt))
    for dirn in range(2):
      for hd in range(N_HEADS):
        st_s[dirn, hd] = s0_ref[dirn, hd].T

    def body(cj, carry):
      run(scan_units([c for slot in range(CHUNK_SLOTS)
                      for c in chunk_pair(0, cj * CHUNK_SLOTS + slot, False, slot)]))
      return carry

    lax.fori_loop(0, cps // CHUNK_SLOTS, body, 0)
    pool2d()
    for r0 in row_blocks(0, tt):
      run(phase4_units(r0))
  else:
    assert cps <= CHUNK_SLOTS and not grid2d

    def post_units(sq):
      return [u for r0 in row_blocks(sq * seq_len, seq_len)
              for u in pool1d_units(r0) + phase4_units(r0)]

    run(phase1_units(0, seq_len))
    for sq in range(n_seq):
      main = scan_units([c for ci in range(cps)
                         for c in chunk_pair(sq * seq_len, ci, ci == 0, ci)])
      if emit_state:
        main.append(functools.partial(emit_states, sq))
      side = post_units(sq - 1) if sq > 0 else []
      if sq + 1 < n_seq:
        side += phase1_units((sq + 1) * seq_len, seq_len)
      run(interleave(main, side))
    run(post_units(n_seq - 1))


def _const_spec(shape):
  nd = len(shape)
  return pl.BlockSpec(shape, lambda i, _nd=nd: (0,) * _nd, pipeline_mode=pl.Buffered(1))


def _mix_call(x2d, mod3, mod_row, lbl, n1, win, ng, wa, pw, ps, wb, wo, s0, *, layer, n_seq,
              seq_len, grid2d, emit_state):
  tokens = x2d.shape[0]
  tt = n_seq * seq_len
  assert tokens % tt == 0 and seq_len % (CHUNK * CHUNK_SLOTS) == 0 and tt % ROWS == 0
  steps = tokens // tt
  has_state = s0 is not None
  x_mode = pl.Buffered(1) if steps <= 2 else None
  in_specs = [
      pl.BlockSpec((tt, D_MODEL), lambda i: (i, 0), pipeline_mode=x_mode),
      pl.BlockSpec((None, 1, 6 * D_MODEL), lambda i: (mod_row(i), 0, 0)),
      _const_spec(lbl.shape), _const_spec(n1.shape), _const_spec(win.shape), _const_spec(ng.shape),
      _const_spec(wa.shape), _const_spec(pw.shape), _const_spec(ps.shape), _const_spec(wb.shape),
      _const_spec(wo.shape),
  ]
  args = [x2d, mod3, lbl, n1, win, ng, wa, pw, ps, wb, wo]
  if has_state:
    in_specs.append(pl.BlockSpec((None, None, 2, N_HEADS, HEAD, HEAD),
                                 lambda i: (i, layer, 0, 0, 0, 0)))
    args.append(s0)
  out_specs = [pl.BlockSpec((tt, D_MODEL), lambda i: (i, 0))]
  out_shape = [jax.ShapeDtypeStruct((tokens, D_MODEL), F32)]
  if emit_state:
    out_specs.append(pl.BlockSpec((n_seq, 2, N_HEADS, HEAD, HEAD), lambda i: (i, 0, 0, 0, 0)))
    out_shape.append(jax.ShapeDtypeStruct((tokens // seq_len, 2, N_HEADS, HEAD, HEAD), F32))
  scratch = [
      pltpu.VMEM((tt, D_MODEL), BF16),
      pltpu.VMEM((tt, D_A), F32),
      pltpu.VMEM((tt, 2 * D_A), F32),
      pltpu.VMEM((tt, D_A), BF16),
      pltpu.VMEM((tt, D_A), BF16),
      pltpu.VMEM((tt, D_B), F32),
      pltpu.VMEM((tt, D_MODEL), BF16),
      pltpu.VMEM((tt, D_MODEL), BF16),
      pltpu.VMEM((2 * CHUNK_SLOTS, N_HEADS, CHUNK, HEAD), F32),
      pltpu.VMEM((2 * CHUNK_SLOTS, CHUNK, D_A), F32),
      pltpu.VMEM((2, tt, D_A), F32),
      pltpu.VMEM((tt, D_B), BF16),
      pltpu.VMEM((2, N_HEADS, HEAD, HEAD), F32),
      pltpu.VMEM((2, CHUNK, CHUNK), BF16),
      pltpu.VMEM((2, CHUNK, CHUNK), jnp.int32),
      pltpu.VMEM((N_GROUPS, ROWS, ROWS), BF16),
      pltpu.VMEM((N_GROUPS, ROWS, 1), F32),
      pltpu.VMEM((2 * CHUNK_SLOTS, N_HEADS, 2 * N_UP + 2, N_BLOCKS, HEAD), F32),
  ]
  if grid2d:
    scratch.append(pltpu.VMEM((tt, D_B), F32))
  body = functools.partial(_mix_body, layer=layer, n_seq=n_seq, seq_len=seq_len, grid2d=grid2d,
                           has_state=has_state, emit_state=emit_state)
  return pl.pallas_call(
      body,
      grid=(steps,),
      in_specs=in_specs,
      out_specs=out_specs,
      out_shape=out_shape,
      scratch_shapes=scratch,
      compiler_params=pltpu.CompilerParams(
          dimension_semantics=("arbitrary",), vmem_limit_bytes=VMEM_LIMIT_V7X),
      name="mix_grid2d" if grid2d else "mix_seq1d",
  )(*args)


FFN_COLS = 256


WI_CHUNK = 512
WO_CHUNK = 256


def _stream_cast(w_hbm, w_s, stage, sems, chunks):
  def copy(k):
    return pltpu.make_async_copy(w_hbm.at[chunks[k]], stage.at[k % 2], sems.at[k % 2])

  copy(0).start()
  for k in range(len(chunks)):
    if k + 1 < len(chunks):
      copy(k + 1).start()
    copy(k).wait()
    w_s[chunks[k]] = stage[k % 2].astype(BF16)


def _ffn_body(xc_ref, xl_ref, mod_ref, n2_ref, wi_hbm, wo_hbm, fg_ref, oc_ref, ol_ref, h_s, hid_s,
              y_s, wi_s, wo_s, sti_s, sto_s, semi, semo, *, ctx_steps):
  tt = xc_ref.shape[0]
  step = pl.program_id(0)
  is_ctx = step < ctx_steps

  @pl.when(step == 0)
  def _():
    _stream_cast(wi_hbm, wi_s, sti_s, semi,
                 [(slice(None), pl.ds(c, WI_CHUNK)) for c in range(0, 2 * D_FF, WI_CHUNK)])
    _stream_cast(wo_hbm, wo_s, sto_s, semo,
                 [(pl.ds(r, WO_CHUNK), slice(None)) for r in range(0, D_FF, WO_CHUNK)])

  mod = mod_ref[...]
  sh2 = mod[:, 3 * D_MODEL:4 * D_MODEL]
  a2 = n2_ref[...] * (1.0 + mod[:, 4 * D_MODEL:5 * D_MODEL])
  gate2 = mod[:, 5 * D_MODEL:6 * D_MODEL]

  def norm_in(x_ref):
    for i in range(tt // ROWS):
      r = slice(i * ROWS, (i + 1) * ROWS)
      h_s[r, :] = (_rms_rows(x_ref[r, :]) * a2 + sh2).astype(BF16)

  pl.when(is_ctx)(lambda: norm_in(xc_ref))
  pl.when(jnp.logical_not(is_ctx))(lambda: norm_in(xl_ref))
  h = h_s[...]
  for j in range(D_FF // FFN_COLS):
    cg = slice(j * FFN_COLS, (j + 1) * FFN_COLS)
    cu = slice(D_FF + j * FFN_COLS, D_FF + (j + 1) * FFN_COLS)
    gp = _dot(h, wi_s[:, cg])
    up = _dot(h, wi_s[:, cu])
    hid_s[:, cg] = (_silu(gp) * up).astype(BF16)
  y_s[...] = _dot(hid_s[...], wo_s[...])
  fg = fg_ref[...]

  def finish(x_ref, o_ref):
    for i in range(tt // ROWS):
      r = slice(i * ROWS, (i + 1) * ROWS)
      o_ref[r, :] = _rms_rows(x_ref[r, :] + gate2 * y_s[r, :]) * fg

  pl.when(is_ctx)(lambda: finish(xc_ref, oc_ref))
  pl.when(jnp.logical_not(is_ctx))(lambda: finish(xl_ref, ol_ref))


def _ffn_call(x_ctx, x_lat, mod3, n2, wi, wo, fg, *, tt, lat_seq_len):
  ctx_tokens, lat_tokens = x_ctx.shape[0], x_lat.shape[0]
  assert ctx_tokens % tt == 0 and lat_tokens % tt == 0 and lat_seq_len % tt == 0 and tt % ROWS == 0
  assert D_FF % FFN_COLS == 0 and (2 * D_FF) % WI_CHUNK == 0 and D_FF % WO_CHUNK == 0
  nc = ctx_tokens // tt
  steps_per_seq = lat_seq_len // tt
  mod_row = lambda i: jnp.where(i < nc, 0, 1 + (i - nc) // steps_per_seq)
  ctx_blk = lambda i: (jnp.minimum(i, nc - 1), 0)
  lat_blk = lambda i: (jnp.maximum(i - nc, 0), 0)
  body = functools.partial(_ffn_body, ctx_steps=nc)
  return pl.pallas_call(
      body,
      grid=((ctx_tokens + lat_tokens) // tt,),
      in_specs=[
          pl.BlockSpec((tt, D_MODEL), ctx_blk),
          pl.BlockSpec((tt, D_MODEL), lat_blk),
          pl.BlockSpec((None, 1, 6 * D_MODEL), lambda i: (mod_row(i), 0, 0)),
          _const_spec(n2.shape),
          pl.BlockSpec(memory_space=pl.ANY),
          pl.BlockSpec(memory_space=pl.ANY),
          _const_spec(fg.shape),
      ],
      out_specs=[
          pl.BlockSpec((tt, D_MODEL), ctx_blk),
          pl.BlockSpec((tt, D_MODEL), lat_blk),
      ],
      out_shape=[jax.ShapeDtypeStruct((ctx_tokens, D_MODEL), F32),
                 jax.ShapeDtypeStruct((lat_tokens, D_MODEL), F32)],
      scratch_shapes=[
          pltpu.VMEM((tt, D_MODEL), BF16),
          pltpu.VMEM((tt, D_FF), BF16),
          pltpu.VMEM((tt, D_MODEL), F32),
          pltpu.VMEM((D_MODEL, 2 * D_FF), BF16),
          pltpu.VMEM((D_FF, D_MODEL), BF16),
          pltpu.VMEM((2, D_MODEL, WI_CHUNK), F32),
          pltpu.VMEM((2, WO_CHUNK, D_MODEL), F32),
          pltpu.SemaphoreType.DMA((2,)),
          pltpu.SemaphoreType.DMA((2,)),
      ],
      compiler_params=pltpu.CompilerParams(
          dimension_semantics=("arbitrary",), vmem_limit_bytes=VMEM_LIMIT_V7X),
      name="ffn_final",
  )(x_ctx, x_lat, mod3, n2, wi, wo, fg)


def kernel(x_prompt, x_sample, state_hgrn, c, c_ctx, w_ada, b_ada, norm1_g, w_in, hgrn_lb_logits,
           hgrn_norm_g, w_branch_a, pool_w, pool_scale, w_branch_b, w_out, norm2_g, w_ffn_in,
           w_ffn_out, final_g):
  bp, sp, _ = x_prompt.shape
  bs, ss, _ = x_sample.shape
  depth = w_in.shape[0]
  assert bs + 1 <= 8 and ss % GRID_W == 0
  xp = x_prompt.reshape(bp * sp, D_MODEL)
  xs = x_sample.reshape(bs * ss, D_MODEL)
  cvec = jnp.concatenate([c_ctx[None, :], c, jnp.zeros((8 - 1 - bs, D_MODEL), F32)], axis=0)
  row2 = lambda a: a.reshape(1, -1)
  ffn_tt = 512
  ctx_states = []
  for l in range(depth):
    mod3 = _mod_call(cvec, w_ada[l], row2(b_ada[l]))[:1 + bs, None, :]
    mix_w = (hgrn_lb_logits, row2(norm1_g[l]), w_in[l].astype(BF16), row2(hgrn_norm_g[l]),
             w_branch_a[l].astype(BF16), pool_w[l].astype(BF16), row2(pool_scale[l]),
             w_branch_b[l].astype(BF16), w_out[l].astype(BF16))
    assert l == depth - 1, "the FFN call also applies the final norm"
    xp, st = _mix_call(xp, mod3, lambda i: 0, *mix_w, None, layer=l, n_seq=2, seq_len=sp,
                       grid2d=False, emit_state=True)
    ctx_states.append(st)
    (xs,) = _mix_call(xs, mod3, lambda i: 1 + i, *mix_w, state_hgrn, layer=l, n_seq=1,
                      seq_len=ss, grid2d=True, emit_state=False)
    xp, xs = _ffn_call(xp, xs, mod3, row2(norm2_g[l]), w_ffn_in[l], w_ffn_out[l], row2(final_g),
                       tt=ffn_tt, lat_seq_len=ss)
  new_state = jnp.stack(ctx_states, axis=1).astype(x_prompt.dtype)
  return (xp.reshape(bp, sp, D_MODEL), xs.reshape(bs, ss, D_MODEL), new_state)
```

```python
import functools

import jax
import jax.numpy as jnp
from jax import lax
from jax.experimental import pallas as pl
from jax.experimental.pallas import tpu as pltpu

F32 = jnp.float32
BF16 = jnp.bfloat16

D_MODEL = 1024
D_A = 512
N_HEADS = 4
HEAD = D_A // N_HEADS
D_B = 512
N_GROUPS = 4
GROUP = D_B // N_GROUPS
POOL_WINDOWS = (2, 4, 8, 16)
GRID_W = 64
D_FF = 2816
IN_COLS = 5 * D_A + D_B + 2 * D_MODEL
EPS = 1e-6

CHUNK = 128
DIAG = 16
CHUNK_SLOTS = 2
PROJ_COLS = 256
N_SCAN_IN = 4
ROWS = 256
EXP_CLAMP = 80.0
VMEM_LIMIT_V7X = 58 * 1024 * 1024


def _sigmoid(x):
  return 0.5 + 0.5 * jnp.tanh(0.5 * x)


def _silu(x):
  h = 0.5 * x
  return h + h * jnp.tanh(h)


def _dot(a, b):
  return jnp.dot(a, b, preferred_element_type=F32)


def _dot_nt(a, b):
  return lax.dot_general(a, b, (((1,), (1,)), ((), ())), preferred_element_type=F32)


def _dot_tn(a, b):
  return lax.dot_general(a, b, (((0,), (0,)), ((), ())), preferred_element_type=F32)


def _split_bf16(x):
  hi = x.astype(BF16)
  lo = (x - hi.astype(F32)).astype(BF16)
  return hi, lo


def _rms_rows(x):
  return x * lax.rsqrt(jnp.mean(x * x, axis=-1, keepdims=True) + EPS)


PREP_STEPS = 8


def _mod_body(c_ref, w_ref, b_ref, *refs):
  n_w = (len(refs) - 1) // 2
  o_ref = refs[n_w]
  cv = c_ref[...]
  s = _silu(cv).astype(BF16)
  o_ref[...] = _dot(s, w_ref[...].astype(BF16)) + b_ref[...]
  for src, dst in zip(refs[:n_w], refs[n_w + 1:]):
    dst[...] = src[...].astype(BF16)


def _mod_call(cvec, w_ada, b_ada, mats, stacks):
  n = w_ada.shape[1]
  assert n % (PREP_STEPS * 128) == 0
  bn = n // PREP_STEPS
  col = lambda j: (0, j)
  in_specs = [pl.BlockSpec((8, D_MODEL), lambda j: (0, 0)),
              pl.BlockSpec((D_MODEL, bn), col),
              pl.BlockSpec((1, bn), col)]
  w_specs, w_shapes = [], []
  for w in mats:
    rows, cols = w.shape
    assert cols % (PREP_STEPS * 128) == 0
    w_specs.append(pl.BlockSpec((rows, cols // PREP_STEPS), col))
    w_shapes.append(jax.ShapeDtypeStruct(w.shape, BF16))
  for w in stacks:
    g, k, m = w.shape
    assert PREP_STEPS % g == 0
    per = PREP_STEPS // g
    w_specs.append(pl.BlockSpec((None, k, m), lambda j, per=per: (j // per, 0, 0)))
    w_shapes.append(jax.ShapeDtypeStruct(w.shape, BF16))
  return pl.pallas_call(
      _mod_body,
      grid=(PREP_STEPS,),
      in_specs=in_specs + w_specs,
      out_specs=[pl.BlockSpec((8, bn), col)] + w_specs,
      out_shape=[jax.ShapeDtypeStruct((8, n), F32)] + w_shapes,
      compiler_params=pltpu.CompilerParams(
          dimension_semantics=("arbitrary",), vmem_limit_bytes=VMEM_LIMIT_V7X),
      name="adaln_mod",
  )(cvec, w_ada, b_ada, *mats, *stacks)


N_BLOCKS = CHUNK // DIAG
N_UP = N_BLOCKS.bit_length() - 1


def _aligned(row):
  return row if isinstance(row, int) else pl.multiple_of(row, CHUNK)


def _query_blocks(level, dirn):
  return [blk for blk in range(N_BLOCKS) if ((blk >> (level - 1)) & 1) != dirn]


def _level_row(level, block, dirn):
  half = DIAG << (level - 1)
  start = (block >> level) * 2 * half
  return start + (half - 1 if dirn == 0 else half)


def _mix_body(*refs, layer, n_seq, seq_len, grid2d, has_state, emit_state):
  (x_ref, mod_ref, lbl_ref, n1_ref, win_ref, ng_ref, wa_ref, pw_ref, ps_ref, wb_ref,
   wo_ref) = refs[:11]
  rest = list(refs[11:])
  s0_ref = rest.pop(0) if has_state else None
  out_ref = rest.pop(0)
  st_ref = rest.pop(0) if emit_state else None
  (h_s, q_s, zf_s, v_s, og_s, zp_s, ga_s, gb_s, b_c, k_c, o_s, pm_s, st_s, tri_s, lev_s,
   band_s, icnt_s, tab_s) = rest[:18]
  cp_s = rest[18] if grid2d else None

  tt = n_seq * seq_len
  cps = seq_len // CHUNK

  ti = lax.broadcasted_iota(jnp.int32, (CHUNK, CHUNK), 0)
  si = lax.broadcasted_iota(jnp.int32, (CHUNK, CHUNK), 1)
  xr = ti ^ si
  lvl = jnp.zeros((CHUNK, CHUNK), jnp.int32)
  for j in range(1, N_UP + 1):
    lvl = jnp.where(xr >= (DIAG << (j - 1)), j, lvl)
  lev_s[0] = jnp.where(ti >= si, lvl, -1)
  lev_s[1] = jnp.where(ti <= si, lvl, -1)
  tri_s[0] = (ti >= si).astype(BF16)
  tri_s[1] = (ti <= si).astype(BF16)

  seg_shift = (GRID_W if grid2d else ROWS).bit_length() - 1
  tr = lax.broadcasted_iota(jnp.int32, (ROWS, ROWS), 0)
  sr = lax.broadcasted_iota(jnp.int32, (ROWS, ROWS), 1)
  same_seg = (tr >> seg_shift) == (sr >> seg_shift)
  for g, w in enumerate(POOL_WINDOWS):
    dlt = sr - tr
    band = (same_seg & (dlt >= -(w // 2)) & (dlt <= w // 2 - 1)).astype(F32)
    band_s[g] = band.astype(BF16)
    icnt_s[g] = 1.0 / jnp.sum(band, axis=1, keepdims=True)

  mod = mod_ref[...]
  sh1 = mod[:, 0:D_MODEL]
  a1 = n1_ref[...] * (1.0 + mod[:, D_MODEL:2 * D_MODEL])
  gate1 = mod[:, 2 * D_MODEL:3 * D_MODEL]
  lg = lbl_ref[...]
  e = jnp.exp(lg - jnp.max(lg, axis=0, keepdims=True))
  lb = jnp.sum(e[:layer + 1], axis=0) / jnp.sum(e, axis=0)

  def run(units):
    for u in units:
      u()

  def interleave(main, side):
    out, taken = [], 0
    for k, u in enumerate(main):
      out.append(u)
      want = ((k + 1) * len(side)) // len(main)
      out.extend(side[taken:want])
      taken = want
    return out

  def phase1_units(r0, n):
    rr = slice(r0, r0 + n)

    def norm():
      for i in range(n // ROWS):
        r = slice(r0 + i * ROWS, r0 + (i + 1) * ROWS)
        h_s[r, :] = (_rms_rows(x_ref[r, :]) * a1 + sh1).astype(BF16)

    def proj(c0, c1):
      return _dot(h_s[rr, :], win_ref[:, c0:c1])

    def put(dst, cols, c0, act, dtype):
      def unit():
        z = proj(c0, c0 + (cols.stop - cols.start))
        dst[rr, cols] = (z if act is None else act(z)).astype(dtype)
      return unit

    layout = [(q_s, 0, D_A, None, F32), (zf_s, 0, D_A, None, F32), (zf_s, D_A, D_A, None, F32),
              (v_s, 0, D_A, None, BF16), (og_s, 0, D_A, _silu, BF16), (zp_s, 0, D_B, None, F32),
              (ga_s, 0, D_MODEL, _sigmoid, BF16), (gb_s, 0, D_MODEL, _sigmoid, BF16)]
    units, n_scan_in, c0 = [norm], None, 0
    for k, (dst, d0, width, act, dtype) in enumerate(layout):
      for j in range(0, width, PROJ_COLS):
        units.append(put(dst, slice(d0 + j, d0 + j + PROJ_COLS), c0 + j, act, dtype))
      c0 += width
      if k + 1 == N_SCAN_IN:
        n_scan_in = len(units)
    return units, n_scan_in

  half_c = 0.5 * (1.0 - lb)
  blk_i = lax.broadcasted_iota(jnp.int32, (N_BLOCKS, HEAD), 0)
  zero_tile = jnp.zeros((DIAG, HEAD), BF16)
  t_eb, t_et = 2 * N_UP, 2 * N_UP + 1

  blk_rows = lambda a, blk: a[blk * DIAG:(blk + 1) * DIAG]

  def stage_gates(base, dirn, slot):
    sd = 2 * slot + dirn
    c1 = half_c[dirn:dirn + 1, :]
    m = c1 * jnp.tanh(0.5 * zf_s[pl.ds(base, CHUNK), dirn * D_A:(dirn + 1) * D_A])
    ghi, glo = _split_bf16(jnp.log((1.0 - c1) + m))
    tri = tri_s[dirn]
    bsum = _dot(tri, ghi) + _dot(tri, glo)
    for hd in range(N_HEADS):
      b_c[sd, hd] = bsum[:, hd * HEAD:(hd + 1) * HEAD]
    k_c[sd] = c1 - m

  def stage_factors(base, dirn, zero_state, slot, hd):
    sd = 2 * slot + dirn
    cs = slice(hd * HEAD, (hd + 1) * HEAD)
    crow = DIAG // 2 if dirn == 0 else DIAG // 2 - 1
    tot_row = CHUNK - 1 if dirn == 0 else 0
    brow = lambda r: b_c[sd, hd, r:r + 1, :]
    tot = brow(tot_row)
    beta = b_c[sd, hd, pl.ds(crow, N_BLOCKS, stride=DIAG), :]
    for l in range(1, N_UP + 1):
      n_par = N_BLOCKS >> l
      ref = brow(_level_row(l, (n_par - 1) << l, dirn))
      for p in range(n_par - 2, -1, -1):
        ref = jnp.where(blk_i < ((p + 1) << l), brow(_level_row(l, p << l, dirn)), ref)
      d = beta - ref
      tab_s[sd, hd, l - 1] = jnp.exp(jnp.minimum(d, 0.0))
      tab_s[sd, hd, N_UP + l - 1] = jnp.exp(jnp.minimum(-d, 0.0))
    tab_s[sd, hd, t_eb] = jnp.exp(beta)
    tab_s[sd, hd, t_et] = jnp.exp(tot - beta)
    trow = lambda t, blk: jnp.broadcast_to(tab_s[sd, hd, t, blk:blk + 1, :], (DIAG, HEAD))

    beta_b = jnp.concatenate(
        [jnp.broadcast_to(brow(blk * DIAG + crow), (DIAG, HEAD)) for blk in range(N_BLOCKS)],
        axis=0)
    de = b_c[sd, hd] - beta_b
    a0 = q_s[pl.ds(base, CHUNK), cs] * jnp.exp(jnp.minimum(de, EXP_CLAMP))
    b0 = k_c[sd, :, cs] * jnp.exp(jnp.minimum(-de, EXP_CLAMP))
    ops = dict(base=base, dirn=dirn, hd=hd, zero_state=zero_state, tot=tot,
               a0=a0.astype(BF16), b0=b0.astype(BF16), lhs={}, rhs={})
    for l in range(1, N_UP + 1):
      qb = _query_blocks(l, dirn)
      ops["lhs"][l] = jnp.concatenate(
          [(blk_rows(a0, blk) * trow(l - 1, blk)).astype(BF16) for blk in qb], axis=0)
      ops["rhs"][l] = jnp.concatenate(
          [zero_tile if blk in qb else (blk_rows(b0, blk) * trow(N_UP + l - 1, blk)).astype(BF16)
           for blk in range(N_BLOCKS)], axis=0)
    ops["kt"] = jnp.concatenate(
        [(blk_rows(b0, blk) * trow(t_et, blk)).astype(BF16) for blk in range(N_BLOCKS)], axis=0)
    if not zero_state:
      ops["qt"] = jnp.concatenate(
          [(blk_rows(a0, blk) * trow(t_eb, blk)).astype(BF16) for blk in range(N_BLOCKS)], axis=0)
    return ops

  def stage_scores(ops):
    dirn = ops["dirn"]
    p_same = _dot_nt(ops["a0"], ops["b0"])
    p_lev = {l: _dot_nt(ops["lhs"][l], ops["rhs"][l]) for l in range(1, N_UP + 1)}
    sc_rows = []
    for blk in range(N_BLOCKS):
      lev = lev_s[dirn, blk * DIAG:(blk + 1) * DIAG, :]
      r = jnp.where(lev == 0, blk_rows(p_same, blk), 0.0)
      for l in range(1, N_UP + 1):
        qb = _query_blocks(l, dirn)
        if blk in qb:
          r = jnp.where(lev == l, blk_rows(p_lev[l], qb.index(blk)), r)
      sc_rows.append(r.astype(BF16))
    ops["sc"] = jnp.concatenate(sc_rows, axis=0)

  def stage_out(ops):
    dirn, hd = ops["dirn"], ops["hd"]
    rows = pl.ds(ops["base"], CHUNK)
    cs = slice(hd * HEAD, (hd + 1) * HEAD)
    vh = v_s[rows, cs]
    o = _dot(ops["sc"], vh)
    if ops["zero_state"]:
      st_s[dirn, hd] = _dot_tn(vh, ops["kt"])
    else:
      st = st_s[dirn, hd]
      o = o + _dot_nt(ops["qt"], st.astype(BF16))
      st_s[dirn, hd] = st * jnp.exp(ops["tot"]) + _dot_tn(vh, ops["kt"])
    o_s[dirn, rows, cs] = o

  def scan_units(chunks):
    group = []
    units = [functools.partial(stage_gates, base, dirn, slot) for base, dirn, _, slot in chunks]
    for base, dirn, zero_state, slot in chunks:
      for hd in range(N_HEADS):
        units.append(lambda a=(base, dirn, zero_state, slot, hd): group.append(stage_factors(*a)))
    n = len(chunks) * N_HEADS
    units += [lambda k=k: stage_scores(group[k]) for k in range(n)]
    units += [lambda k=k: stage_out(group[k]) for k in range(n)]
    return units

  def chunk_pair(seq_base, ci, zero_state, slot):
    return [(_aligned(seq_base + ci * CHUNK), 0, zero_state, slot),
            (_aligned(seq_base + (cps - 1 - ci) * CHUNK), 1, zero_state, slot)]

  def emit_states(sq):
    for dirn in range(2):
      for hd in range(N_HEADS):
        st_ref[sq, dirn, hd] = st_s[dirn, hd].T

  def band_pool(r, g):
    gs = slice(g * GROUP, (g + 1) * GROUP)
    zhi, zlo = _split_bf16(zp_s[r, gs])
    return (_dot(band_s[g], zhi) + _dot(band_s[g], zlo)) * icnt_s[g]

  def pool1d_units(r0):
    r = slice(r0, r0 + ROWS)

    def unit(g):
      gs = slice(g * GROUP, (g + 1) * GROUP)
      pm_s[r, gs] = (band_pool(r, g) - zp_s[r, gs]).astype(BF16)
    return [functools.partial(unit, g) for g in range(N_GROUPS)]

  def pool2d():
    n_rows = seq_len // GRID_W
    for i in range(tt // ROWS):
      r = slice(i * ROWS, (i + 1) * ROWS)
      for g in range(N_GROUPS):
        cp_s[r, g * GROUP:(g + 1) * GROUP] = band_pool(r, g)
    for g, w in enumerate(POOL_WINDOWS):
      gs = slice(g * GROUP, (g + 1) * GROUP)
      acc = None
      lo_prev = hi_prev = None
      for rr in range(n_rows):
        lo = max(rr - w // 2, 0)
        hi = min(rr + w // 2 - 1, n_rows - 1)
        if rr == 0:
          acc = cp_s[0:GRID_W, gs]
          for k in range(1, hi + 1):
            acc = acc + cp_s[k * GRID_W:(k + 1) * GRID_W, gs]
        else:
          if hi > hi_prev:
            acc = acc + cp_s[hi * GRID_W:(hi + 1) * GRID_W, gs]
          if lo > lo_prev:
            acc = acc - cp_s[lo_prev * GRID_W:(lo_prev + 1) * GRID_W, gs]
        lo_prev, hi_prev = lo, hi
        rs = slice(rr * GRID_W, (rr + 1) * GRID_W)
        pm_s[rs, gs] = (acc * (1.0 / (hi - lo + 1)) - zp_s[rs, gs]).astype(BF16)

  ng = ng_ref[...]
  psc = ps_ref[...]

  def phase4_units(r0):
    r = slice(r0, r0 + ROWS)
    vals = {}

    def branch_a():
      parts = []
      for hd in range(N_HEADS):
        cs = slice(hd * HEAD, (hd + 1) * HEAD)
        parts.append(_rms_rows(o_s[0, r, cs] + o_s[1, r, cs]) * ng[:, cs])
      oa = (jnp.concatenate(parts, axis=1) * og_s[r, :].astype(F32)).astype(BF16)
      vals["ya"] = _dot(oa, wa_ref[...])

    def branch_b():
      obp = []
      for g in range(N_GROUPS):
        gs = slice(g * GROUP, (g + 1) * GROUP)
        obp.append(_dot(pm_s[r, gs], pw_ref[g]) * psc[:, gs])
      ob = jnp.concatenate(obp, axis=1).astype(BF16)
      vals["yb"] = _dot(ob, wb_ref[...])

    def merge_out():
      merged = (ga_s[r, :].astype(F32) * vals["ya"]
                + gb_s[r, :].astype(F32) * vals["yb"]).astype(BF16)
      out_ref[r, :] = x_ref[r, :] + gate1 * _dot(merged, wo_ref[...])

    return [branch_a, branch_b, merge_out]

  row_blocks = lambda r0, n: range(r0, r0 + n, ROWS)

  if has_state:
    assert n_seq == 1 and grid2d and not emit_state
    run(phase1_units(0, tt)[0])
    for dirn in range(2):
      for hd in range(N_HEADS):
        st_s[dirn, hd] = s0_ref[dirn, hd].T

    def body(cj, carry):
      run(scan_units([c for slot in range(CHUNK_SLOTS)
                      for c in chunk_pair(0, cj * CHUNK_SLOTS + slot, False, slot)]))
      return carry

    lax.fori_loop(0, cps // CHUNK_SLOTS, body, 0)
    pool2d()
    for r0 in row_blocks(0, tt):
      run(phase4_units(r0))
  else:
    assert cps <= CHUNK_SLOTS and not grid2d
    p1, n_scan_in = zip(*[phase1_units(sq * seq_len, seq_len) for sq in range(n_seq)])
    n_scan_in = n_scan_in[0]
    blocks = lambda sq: row_blocks(sq * seq_len, seq_len)

    run(p1[0][:n_scan_in])
    for sq in range(n_seq):
      main = scan_units([c for ci in range(cps)
                         for c in chunk_pair(sq * seq_len, ci, ci == 0, ci)])
      if emit_state:
        main.append(functools.partial(emit_states, sq))
      side = p1[sq][n_scan_in:]
      if sq + 1 < n_seq:
        side += p1[sq + 1][:n_scan_in]
      side += [u for r0 in blocks(sq) for u in pool1d_units(r0)]
      if sq > 0:
        side += [u for r0 in blocks(sq - 1) for u in phase4_units(r0)]
      run(interleave(main, side))
    run([u for r0 in blocks(n_seq - 1) for u in phase4_units(r0)])


def _const_spec(shape):
  nd = len(shape)
  return pl.BlockSpec(shape, lambda i, _nd=nd: (0,) * _nd, pipeline_mode=pl.Buffered(1))


def _mix_call(x2d, mod3, mod_row, lbl, n1, win, ng, wa, pw, ps, wb, wo, s0, *, layer, n_seq,
              seq_len, grid2d, emit_state):
  tokens = x2d.shape[0]
  tt = n_seq * seq_len
  assert tokens % tt == 0 and seq_len % (CHUNK * CHUNK_SLOTS) == 0 and tt % ROWS == 0
  steps = tokens // tt
  has_state = s0 is not None
  x_mode = pl.Buffered(1) if steps <= 2 else None
  in_specs = [
      pl.BlockSpec((tt, D_MODEL), lambda i: (i, 0), pipeline_mode=x_mode),
      pl.BlockSpec((None, 1, 6 * D_MODEL), lambda i: (mod_row(i), 0, 0)),
      _const_spec(lbl.shape), _const_spec(n1.shape), _const_spec(win.shape), _const_spec(ng.shape),
      _const_spec(wa.shape), _const_spec(pw.shape), _const_spec(ps.shape), _const_spec(wb.shape),
      _const_spec(wo.shape),
  ]
  args = [x2d, mod3, lbl, n1, win, ng, wa, pw, ps, wb, wo]
  if has_state:
    in_specs.append(pl.BlockSpec((None, None, 2, N_HEADS, HEAD, HEAD),
                                 lambda i: (i, layer, 0, 0, 0, 0)))
    args.append(s0)
  out_specs = [pl.BlockSpec((tt, D_MODEL), lambda i: (i, 0))]
  out_shape = [jax.ShapeDtypeStruct((tokens, D_MODEL), F32)]
  if emit_state:
    out_specs.append(pl.BlockSpec((n_seq, 2, N_HEADS, HEAD, HEAD), lambda i: (i, 0, 0, 0, 0)))
    out_shape.append(jax.ShapeDtypeStruct((tokens // seq_len, 2, N_HEADS, HEAD, HEAD), F32))
  scratch = [
      pltpu.VMEM((tt, D_MODEL), BF16),
      pltpu.VMEM((tt, D_A), F32),
      pltpu.VMEM((tt, 2 * D_A), F32),
      pltpu.VMEM((tt, D_A), BF16),
      pltpu.VMEM((tt, D_A), BF16),
      pltpu.VMEM((tt, D_B), F32),
      pltpu.VMEM((tt, D_MODEL), BF16),
      pltpu.VMEM((tt, D_MODEL), BF16),
      pltpu.VMEM((2 * CHUNK_SLOTS, N_HEADS, CHUNK, HEAD), F32),
      pltpu.VMEM((2 * CHUNK_SLOTS, CHUNK, D_A), F32),
      pltpu.VMEM((2, tt, D_A), F32),
      pltpu.VMEM((tt, D_B), BF16),
      pltpu.VMEM((2, N_HEADS, HEAD, HEAD), F32),
      pltpu.VMEM((2, CHUNK, CHUNK), BF16),
      pltpu.VMEM((2, CHUNK, CHUNK), jnp.int32),
      pltpu.VMEM((N_GROUPS, ROWS, ROWS), BF16),
      pltpu.VMEM((N_GROUPS, ROWS, 1), F32),
      pltpu.VMEM((2 * CHUNK_SLOTS, N_HEADS, 2 * N_UP + 2, N_BLOCKS, HEAD), F32),
  ]
  if grid2d:
    scratch.append(pltpu.VMEM((tt, D_B), F32))
  body = functools.partial(_mix_body, layer=layer, n_seq=n_seq, seq_len=seq_len, grid2d=grid2d,
                           has_state=has_state, emit_state=emit_state)
  return pl.pallas_call(
      body,
      grid=(steps,),
      in_specs=in_specs,
      out_specs=out_specs,
      out_shape=out_shape,
      scratch_shapes=scratch,
      compiler_params=pltpu.CompilerParams(
          dimension_semantics=("arbitrary",), vmem_limit_bytes=VMEM_LIMIT_V7X),
      name="mix_grid2d" if grid2d else "mix_seq1d",
  )(*args)


FFN_COLS = 256


WI_CHUNK = 512
WO_CHUNK = 256


def _stream_cast(w_hbm, w_s, stage, sems, chunks):
  def copy(k):
    return pltpu.make_async_copy(w_hbm.at[chunks[k]], stage.at[k % 2], sems.at[k % 2])

  copy(0).start()
  for k in range(len(chunks)):
    if k + 1 < len(chunks):
      copy(k + 1).start()
    copy(k).wait()
    w_s[chunks[k]] = stage[k % 2].astype(BF16)


def _ffn_body(xc_ref, xl_ref, mod_ref, n2_ref, wi_hbm, wo_hbm, fg_ref, oc_ref, ol_ref, h_s, hid_s,
              y_s, wi_s, wo_s, sti_s, sto_s, semi, semo, *, ctx_steps):
  tt = xc_ref.shape[0]
  step = pl.program_id(0)
  is_ctx = step < ctx_steps

  @pl.when(step == 0)
  def _():
    _stream_cast(wi_hbm, wi_s, sti_s, semi,
                 [(slice(None), pl.ds(c, WI_CHUNK)) for c in range(0, 2 * D_FF, WI_CHUNK)])
    _stream_cast(wo_hbm, wo_s, sto_s, semo,
                 [(pl.ds(r, WO_CHUNK), slice(None)) for r in range(0, D_FF, WO_CHUNK)])

  mod = mod_ref[...]
  sh2 = mod[:, 3 * D_MODEL:4 * D_MODEL]
  a2 = n2_ref[...] * (1.0 + mod[:, 4 * D_MODEL:5 * D_MODEL])
  gate2 = mod[:, 5 * D_MODEL:6 * D_MODEL]

  for i in range(tt // ROWS):
    r = slice(i * ROWS, (i + 1) * ROWS)
    x = jnp.where(is_ctx, xc_ref[r, :], xl_ref[r, :])
    h_s[r, :] = (_rms_rows(x) * a2 + sh2).astype(BF16)
  h = h_s[...]
  for j in range(D_FF // FFN_COLS):
    cg = slice(j * FFN_COLS, (j + 1) * FFN_COLS)
    cu = slice(D_FF + j * FFN_COLS, D_FF + (j + 1) * FFN_COLS)
    gp = _dot(h, wi_s[:, cg])
    up = _dot(h, wi_s[:, cu])
    hid_s[:, cg] = (_silu(gp) * up).astype(BF16)
  y_s[...] = _dot(hid_s[...], wo_s[...])
  fg = fg_ref[...]

  def finish(x_ref, o_ref):
    for i in range(tt // ROWS):
      r = slice(i * ROWS, (i + 1) * ROWS)
      o_ref[r, :] = _rms_rows(x_ref[r, :] + gate2 * y_s[r, :]) * fg

  pl.when(is_ctx)(lambda: finish(xc_ref, oc_ref))
  pl.when(jnp.logical_not(is_ctx))(lambda: finish(xl_ref, ol_ref))


def _ffn_call(x_ctx, x_lat, mod3, n2, wi, wo, fg, *, tt, lat_seq_len):
  ctx_tokens, lat_tokens = x_ctx.shape[0], x_lat.shape[0]
  assert ctx_tokens % tt == 0 and lat_tokens % tt == 0 and lat_seq_len % tt == 0 and tt % ROWS == 0
  assert D_FF % FFN_COLS == 0 and (2 * D_FF) % WI_CHUNK == 0 and D_FF % WO_CHUNK == 0
  nc = ctx_tokens // tt
  steps_per_seq = lat_seq_len // tt
  mod_row = lambda i: jnp.where(i < nc, 0, 1 + (i - nc) // steps_per_seq)
  ctx_blk = lambda i: (jnp.minimum(i, nc - 1), 0)
  lat_blk = lambda i: (jnp.maximum(i - nc, 0), 0)
  body = functools.partial(_ffn_body, ctx_steps=nc)
  return pl.pallas_call(
      body,
      grid=((ctx_tokens + lat_tokens) // tt,),
      in_specs=[
          pl.BlockSpec((tt, D_MODEL), ctx_blk),
          pl.BlockSpec((tt, D_MODEL), lat_blk),
          pl.BlockSpec((None, 1, 6 * D_MODEL), lambda i: (mod_row(i), 0, 0)),
          _const_spec(n2.shape),
          pl.BlockSpec(memory_space=pl.ANY),
          pl.BlockSpec(memory_space=pl.ANY),
          _const_spec(fg.shape),
      ],
      out_specs=[
          pl.BlockSpec((tt, D_MODEL), ctx_blk),
          pl.BlockSpec((tt, D_MODEL), lat_blk),
      ],
      out_shape=[jax.ShapeDtypeStruct((ctx_tokens, D_MODEL), F32),
                 jax.ShapeDtypeStruct((lat_tokens, D_MODEL), F32)],
      scratch_shapes=[
          pltpu.VMEM((tt, D_MODEL), BF16),
          pltpu.VMEM((tt, D_FF), BF16),
          pltpu.VMEM((tt, D_MODEL), F32),
          pltpu.VMEM((D_MODEL, 2 * D_FF), BF16),
          pltpu.VMEM((D_FF, D_MODEL), BF16),
          pltpu.VMEM((2, D_MODEL, WI_CHUNK), F32),
          pltpu.VMEM((2, WO_CHUNK, D_MODEL), F32),
          pltpu.SemaphoreType.DMA((2,)),
          pltpu.SemaphoreType.DMA((2,)),
      ],
      compiler_params=pltpu.CompilerParams(
          dimension_semantics=("arbitrary",), vmem_limit_bytes=VMEM_LIMIT_V7X),
      name="ffn_final",
  )(x_ctx, x_lat, mod3, n2, wi, wo, fg)


def kernel(x_prompt, x_sample, state_hgrn, c, c_ctx, w_ada, b_ada, norm1_g, w_in, hgrn_lb_logits,
           hgrn_norm_g, w_branch_a, pool_w, pool_scale, w_branch_b, w_out, norm2_g, w_ffn_in,
           w_ffn_out, final_g):
  bp, sp, _ = x_prompt.shape
  bs, ss, _ = x_sample.shape
  depth = w_in.shape[0]
  assert bs + 1 <= 8 and ss % GRID_W == 0
  xp = x_prompt.reshape(bp * sp, D_MODEL)
  xs = x_sample.reshape(bs * ss, D_MODEL)
  cvec = jnp.concatenate([c_ctx[None, :], c, jnp.zeros((8 - 1 - bs, D_MODEL), F32)], axis=0)
  row2 = lambda a: a.reshape(1, -1)
  ffn_tt = 512
  ctx_states = []
  for l in range(depth):
    mod, win16, wa16, wb16, wo16, pw16 = _mod_call(
        cvec, w_ada[l], row2(b_ada[l]), [w_in[l], w_branch_a[l], w_branch_b[l], w_out[l]],
        [pool_w[l]])
    mod3 = mod[:1 + bs, None, :]
    mix_w = (hgrn_lb_logits, row2(norm1_g[l]), win16, row2(hgrn_norm_g[l]), wa16, pw16,
             row2(pool_scale[l]), wb16, wo16)
    assert l == depth - 1, "the FFN call also applies the final norm"
    xp, st = _mix_call(xp, mod3, lambda i: 0, *mix_w, None, layer=l, n_seq=2, seq_len=sp,
                       grid2d=False, emit_state=True)
    ctx_states.append(st)
    (xs,) = _mix_call(xs, mod3, lambda i: 1 + i, *mix_w, state_hgrn, layer=l, n_seq=1,
                      seq_len=ss, grid2d=True, emit_state=False)
    xp, xs = _ffn_call(xp, xs, mod3, row2(norm2_g[l]), w_ffn_in[l], w_ffn_out[l], row2(final_g),
                       tt=ffn_tt, lat_seq_len=ss)
  new_state = jnp.stack(ctx_states, axis=1).astype(x_prompt.dtype)
  return (xp.reshape(bp, sp, D_MODEL), xs.reshape(bs, ss, D_MODEL), new_state)
```

```python
import functools

import jax
import jax.numpy as jnp
from jax import lax
from jax.experimental import pallas as pl
from jax.experimental.pallas import tpu as pltpu

F32 = jnp.float32
BF16 = jnp.bfloat16

D_MODEL = 1024
D_A = 512
N_HEADS = 4
HEAD = D_A // N_HEADS
D_B = 512
N_GROUPS = 4
GROUP = D_B // N_GROUPS
POOL_WINDOWS = (2, 4, 8, 16)
GRID_W = 64
D_FF = 2816
IN_COLS = 5 * D_A + D_B + 2 * D_MODEL
EPS = 1e-6

CHUNK = 128
DIAG = 16
CHUNK_SLOTS = 2
PROJ_COLS = 256
N_SCAN_IN = 4
ROWS = 256
EXP_CLAMP = 80.0
VMEM_LIMIT_V7X = 58 * 1024 * 1024


def _sigmoid(x):
  return 0.5 + 0.5 * jnp.tanh(0.5 * x)


def _silu(x):
  h = 0.5 * x
  return h + h * jnp.tanh(h)


def _dot(a, b):
  return jnp.dot(a, b, preferred_element_type=F32)


def _dot_nt(a, b):
  return lax.dot_general(a, b, (((1,), (1,)), ((), ())), preferred_element_type=F32)


def _dot_tn(a, b):
  return lax.dot_general(a, b, (((0,), (0,)), ((), ())), preferred_element_type=F32)


def _split_bf16(x):
  hi = x.astype(BF16)
  lo = (x - hi.astype(F32)).astype(BF16)
  return hi, lo


def _rms_rows(x):
  return x * lax.rsqrt(jnp.mean(x * x, axis=-1, keepdims=True) + EPS)


PREP_STEPS = 8


def _mod_body(c_ref, w_ref, b_ref, *refs):
  n_w = (len(refs) - 1) // 2
  o_ref = refs[n_w]
  cv = c_ref[...]
  s = _silu(cv).astype(BF16)
  o_ref[...] = _dot(s, w_ref[...].astype(BF16)) + b_ref[...]
  for src, dst in zip(refs[:n_w], refs[n_w + 1:]):
    dst[...] = src[...].astype(BF16)


def _mod_call(cvec, w_ada, b_ada, mats, stacks):
  n = w_ada.shape[1]
  assert n % (PREP_STEPS * 128) == 0
  bn = n // PREP_STEPS
  col = lambda j: (0, j)
  in_specs = [pl.BlockSpec((8, D_MODEL), lambda j: (0, 0)),
              pl.BlockSpec((D_MODEL, bn), col),
              pl.BlockSpec((1, bn), col)]
  w_specs, w_shapes = [], []
  for w in mats:
    rows, cols = w.shape
    assert cols % (PREP_STEPS * 128) == 0
    w_specs.append(pl.BlockSpec((rows, cols // PREP_STEPS), col))
    w_shapes.append(jax.ShapeDtypeStruct(w.shape, BF16))
  for w in stacks:
    g, k, m = w.shape
    assert PREP_STEPS % g == 0
    per = PREP_STEPS // g
    w_specs.append(pl.BlockSpec((None, k, m), lambda j, per=per: (j // per, 0, 0)))
    w_shapes.append(jax.ShapeDtypeStruct(w.shape, BF16))
  return pl.pallas_call(
      _mod_body,
      grid=(PREP_STEPS,),
      in_specs=in_specs + w_specs,
      out_specs=[pl.BlockSpec((8, bn), col)] + w_specs,
      out_shape=[jax.ShapeDtypeStruct((8, n), F32)] + w_shapes,
      compiler_params=pltpu.CompilerParams(
          dimension_semantics=("arbitrary",), vmem_limit_bytes=VMEM_LIMIT_V7X),
      name="adaln_mod",
  )(cvec, w_ada, b_ada, *mats, *stacks)


N_BLOCKS = CHUNK // DIAG
N_UP = N_BLOCKS.bit_length() - 1


def _aligned(row):
  return row if isinstance(row, int) else pl.multiple_of(row, CHUNK)


def _query_blocks(level, dirn):
  return [blk for blk in range(N_BLOCKS) if ((blk >> (level - 1)) & 1) != dirn]


def _level_row(level, block, dirn):
  half = DIAG << (level - 1)
  start = (block >> level) * 2 * half
  return start + (half - 1 if dirn == 0 else half)


def _mix_body(*refs, layer, n_seq, seq_len, grid2d, has_state, emit_state):
  (x_ref, mod_ref, lbl_ref, n1_ref, win_ref, ng_ref, wa_ref, pw_ref, ps_ref, wb_ref,
   wo_ref) = refs[:11]
  rest = list(refs[11:])
  s0_ref = rest.pop(0) if has_state else None
  out_ref = rest.pop(0)
  st_ref = rest.pop(0) if emit_state else None
  (h_s, q_s, zf_s, v_s, og_s, zp_s, ga_s, gb_s, b_c, k_c, o_s, pm_s, st_s, tri_s, lev_s,
   band_s, icnt_s, tab_s) = rest[:18]
  cp_s = rest[18] if grid2d else None

  tt = n_seq * seq_len
  cps = seq_len // CHUNK

  ti = lax.broadcasted_iota(jnp.int32, (CHUNK, CHUNK), 0)
  si = lax.broadcasted_iota(jnp.int32, (CHUNK, CHUNK), 1)
  xr = ti ^ si
  lvl = jnp.zeros((CHUNK, CHUNK), jnp.int32)
  for j in range(1, N_UP + 1):
    lvl = jnp.where(xr >= (DIAG << (j - 1)), j, lvl)
  lev_s[0] = jnp.where(ti >= si, lvl, -1)
  lev_s[1] = jnp.where(ti <= si, lvl, -1)
  tri_s[0] = (ti >= si).astype(BF16)
  tri_s[1] = (ti <= si).astype(BF16)

  seg_shift = (GRID_W if grid2d else ROWS).bit_length() - 1
  tr = lax.broadcasted_iota(jnp.int32, (ROWS, ROWS), 0)
  sr = lax.broadcasted_iota(jnp.int32, (ROWS, ROWS), 1)
  same_seg = (tr >> seg_shift) == (sr >> seg_shift)
  for g, w in enumerate(POOL_WINDOWS):
    dlt = sr - tr
    band = (same_seg & (dlt >= -(w // 2)) & (dlt <= w // 2 - 1)).astype(F32)
    band_s[g] = band.astype(BF16)
    icnt_s[g] = 1.0 / jnp.sum(band, axis=1, keepdims=True)

  mod = mod_ref[...]
  sh1 = mod[:, 0:D_MODEL]
  a1 = n1_ref[...] * (1.0 + mod[:, D_MODEL:2 * D_MODEL])
  gate1 = mod[:, 2 * D_MODEL:3 * D_MODEL]
  lg = lbl_ref[...]
  e = jnp.exp(lg - jnp.max(lg, axis=0, keepdims=True))
  lb = jnp.sum(e[:layer + 1], axis=0) / jnp.sum(e, axis=0)

  def run(units):
    for u in units:
      u()

  def interleave(main, side):
    out, taken = [], 0
    for k, u in enumerate(main):
      out.append(u)
      want = ((k + 1) * len(side)) // len(main)
      out.extend(side[taken:want])
      taken = want
    return out

  def phase1_units(r0, n):
    rr = slice(r0, r0 + n)

    def norm():
      for i in range(n // ROWS):
        r = slice(r0 + i * ROWS, r0 + (i + 1) * ROWS)
        h_s[r, :] = (_rms_rows(x_ref[r, :]) * a1 + sh1).astype(BF16)

    def proj(c0, c1):
      return _dot(h_s[rr, :], win_ref[:, c0:c1])

    def put(dst, cols, c0, act, dtype):
      def unit():
        z = proj(c0, c0 + (cols.stop - cols.start))
        dst[rr, cols] = (z if act is None else act(z)).astype(dtype)
      return unit

    layout = [(q_s, 0, D_A, None, F32), (zf_s, 0, D_A, None, F32), (zf_s, D_A, D_A, None, F32),
              (v_s, 0, D_A, None, BF16), (og_s, 0, D_A, _silu, BF16), (zp_s, 0, D_B, None, F32),
              (ga_s, 0, D_MODEL, _sigmoid, BF16), (gb_s, 0, D_MODEL, _sigmoid, BF16)]
    units, n_scan_in, c0 = [norm], None, 0
    for k, (dst, d0, width, act, dtype) in enumerate(layout):
      for j in range(0, width, PROJ_COLS):
        units.append(put(dst, slice(d0 + j, d0 + j + PROJ_COLS), c0 + j, act, dtype))
      c0 += width
      if k + 1 == N_SCAN_IN:
        n_scan_in = len(units)
    return units, n_scan_in

  half_c = 0.5 * (1.0 - lb)
  blk_i = lax.broadcasted_iota(jnp.int32, (N_BLOCKS, HEAD), 0)
  zero_tile = jnp.zeros((DIAG, HEAD), BF16)
  t_eb, t_et = 2 * N_UP, 2 * N_UP + 1

  blk_rows = lambda a, blk: a[blk * DIAG:(blk + 1) * DIAG]

  def stage_gates(base, dirn, slot):
    sd = 2 * slot + dirn
    c1 = half_c[dirn:dirn + 1, :]
    m = c1 * jnp.tanh(0.5 * zf_s[pl.ds(base, CHUNK), dirn * D_A:(dirn + 1) * D_A])
    ghi, glo = _split_bf16(jnp.log((1.0 - c1) + m))
    tri = tri_s[dirn]
    bsum = _dot(tri, ghi) + _dot(tri, glo)
    for hd in range(N_HEADS):
      b_c[sd, hd] = bsum[:, hd * HEAD:(hd + 1) * HEAD]
    k_c[sd] = c1 - m

  def stage_factors(base, dirn, zero_state, slot, hd):
    sd = 2 * slot + dirn
    cs = slice(hd * HEAD, (hd + 1) * HEAD)
    crow = DIAG // 2 if dirn == 0 else DIAG // 2 - 1
    tot_row = CHUNK - 1 if dirn == 0 else 0
    brow = lambda r: b_c[sd, hd, r:r + 1, :]
    tot = brow(tot_row)
    beta = b_c[sd, hd, pl.ds(crow, N_BLOCKS, stride=DIAG), :]
    for l in range(1, N_UP + 1):
      n_par = N_BLOCKS >> l
      ref = brow(_level_row(l, (n_par - 1) << l, dirn))
      for p in range(n_par - 2, -1, -1):
        ref = jnp.where(blk_i < ((p + 1) << l), brow(_level_row(l, p << l, dirn)), ref)
      d = beta - ref
      tab_s[sd, hd, l - 1] = jnp.exp(jnp.minimum(d, 0.0))
      tab_s[sd, hd, N_UP + l - 1] = jnp.exp(jnp.minimum(-d, 0.0))
    tab_s[sd, hd, t_eb] = jnp.exp(beta)
    tab_s[sd, hd, t_et] = jnp.exp(tot - beta)
    trow = lambda t, blk: jnp.broadcast_to(tab_s[sd, hd, t, blk:blk + 1, :], (DIAG, HEAD))

    beta_b = jnp.concatenate(
        [jnp.broadcast_to(brow(blk * DIAG + crow), (DIAG, HEAD)) for blk in range(N_BLOCKS)],
        axis=0)
    de = b_c[sd, hd] - beta_b
    a0 = q_s[pl.ds(base, CHUNK), cs] * jnp.exp(jnp.minimum(de, EXP_CLAMP))
    b0 = k_c[sd, :, cs] * jnp.exp(jnp.minimum(-de, EXP_CLAMP))
    ops = dict(base=base, dirn=dirn, hd=hd, zero_state=zero_state, tot=tot,
               a0=a0.astype(BF16), b0=b0.astype(BF16), lhs={}, rhs={})
    for l in range(1, N_UP + 1):
      qb = _query_blocks(l, dirn)
      ops["lhs"][l] = jnp.concatenate(
          [(blk_rows(a0, blk) * trow(l - 1, blk)).astype(BF16) for blk in qb], axis=0)
      ops["rhs"][l] = jnp.concatenate(
          [zero_tile if blk in qb else (blk_rows(b0, blk) * trow(N_UP + l - 1, blk)).astype(BF16)
           for blk in range(N_BLOCKS)], axis=0)
    ops["kt"] = jnp.concatenate(
        [(blk_rows(b0, blk) * trow(t_et, blk)).astype(BF16) for blk in range(N_BLOCKS)], axis=0)
    if not zero_state:
      ops["qt"] = jnp.concatenate(
          [(blk_rows(a0, blk) * trow(t_eb, blk)).astype(BF16) for blk in range(N_BLOCKS)], axis=0)
    return ops

  def stage_scores(ops):
    dirn = ops["dirn"]
    p_same = _dot_nt(ops["a0"], ops["b0"])
    p_lev = {l: _dot_nt(ops["lhs"][l], ops["rhs"][l]) for l in range(1, N_UP + 1)}
    sc_rows = []
    for blk in range(N_BLOCKS):
      lev = lev_s[dirn, blk * DIAG:(blk + 1) * DIAG, :]
      r = jnp.where(lev == 0, blk_rows(p_same, blk), 0.0)
      for l in range(1, N_UP + 1):
        qb = _query_blocks(l, dirn)
        if blk in qb:
          r = jnp.where(lev == l, blk_rows(p_lev[l], qb.index(blk)), r)
      sc_rows.append(r.astype(BF16))
    ops["sc"] = jnp.concatenate(sc_rows, axis=0)

  def stage_out(ops):
    dirn, hd = ops["dirn"], ops["hd"]
    rows = pl.ds(ops["base"], CHUNK)
    cs = slice(hd * HEAD, (hd + 1) * HEAD)
    vh = v_s[rows, cs]
    o = _dot(ops["sc"], vh)
    if ops["zero_state"]:
      st_s[dirn, hd] = _dot_tn(vh, ops["kt"])
    else:
      st = st_s[dirn, hd]
      o = o + _dot_nt(ops["qt"], st.astype(BF16))
      st_s[dirn, hd] = st * jnp.exp(ops["tot"]) + _dot_tn(vh, ops["kt"])
    o_s[dirn, rows, cs] = o

  def scan_units(chunks):
    group = []
    units = [functools.partial(stage_gates, base, dirn, slot) for base, dirn, _, slot in chunks]
    for base, dirn, zero_state, slot in chunks:
      for hd in range(N_HEADS):
        units.append(lambda a=(base, dirn, zero_state, slot, hd): group.append(stage_factors(*a)))
    n = len(chunks) * N_HEADS
    units += [lambda k=k: stage_scores(group[k]) for k in range(n)]
    units += [lambda k=k: stage_out(group[k]) for k in range(n)]
    return units

  def chunk_pair(seq_base, ci, zero_state, slot):
    return [(_aligned(seq_base + ci * CHUNK), 0, zero_state, slot),
            (_aligned(seq_base + (cps - 1 - ci) * CHUNK), 1, zero_state, slot)]

  def emit_states(sq):
    for dirn in range(2):
      for hd in range(N_HEADS):
        st_ref[sq, dirn, hd] = st_s[dirn, hd].T

  def band_pool(r, g):
    gs = slice(g * GROUP, (g + 1) * GROUP)
    zhi, zlo = _split_bf16(zp_s[r, gs])
    return (_dot(band_s[g], zhi) + _dot(band_s[g], zlo)) * icnt_s[g]

  def pool1d_units(r0):
    r = slice(r0, r0 + ROWS)

    def unit(g):
      gs = slice(g * GROUP, (g + 1) * GROUP)
      pm_s[r, gs] = (band_pool(r, g) - zp_s[r, gs]).astype(BF16)
    return [functools.partial(unit, g) for g in range(N_GROUPS)]

  def pool2d():
    n_rows = seq_len // GRID_W
    for i in range(tt // ROWS):
      r = slice(i * ROWS, (i + 1) * ROWS)
      for g in range(N_GROUPS):
        cp_s[r, g * GROUP:(g + 1) * GROUP] = band_pool(r, g)
    for g, w in enumerate(POOL_WINDOWS):
      gs = slice(g * GROUP, (g + 1) * GROUP)
      acc = None
      lo_prev = hi_prev = None
      for rr in range(n_rows):
        lo = max(rr - w // 2, 0)
        hi = min(rr + w // 2 - 1, n_rows - 1)
        if rr == 0:
          acc = cp_s[0:GRID_W, gs]
          for k in range(1, hi + 1):
            acc = acc + cp_s[k * GRID_W:(k + 1) * GRID_W, gs]
        else:
          if hi > hi_prev:
            acc = acc + cp_s[hi * GRID_W:(hi + 1) * GRID_W, gs]
          if lo > lo_prev:
            acc = acc - cp_s[lo_prev * GRID_W:(lo_prev + 1) * GRID_W, gs]
        lo_prev, hi_prev = lo, hi
        rs = slice(rr * GRID_W, (rr + 1) * GRID_W)
        pm_s[rs, gs] = (acc * (1.0 / (hi - lo + 1)) - zp_s[rs, gs]).astype(BF16)

  ng = ng_ref[...]
  psc = ps_ref[...]

  def phase4_units(r0):
    r = slice(r0, r0 + ROWS)
    vals = {}

    def branch_a():
      parts = []
      for hd in range(N_HEADS):
        cs = slice(hd * HEAD, (hd + 1) * HEAD)
        parts.append(_rms_rows(o_s[0, r, cs] + o_s[1, r, cs]) * ng[:, cs])
      oa = (jnp.concatenate(parts, axis=1) * og_s[r, :].astype(F32)).astype(BF16)
      vals["ya"] = _dot(oa, wa_ref[...])

    def branch_b():
      obp = []
      for g in range(N_GROUPS):
        gs = slice(g * GROUP, (g + 1) * GROUP)
        obp.append(_dot(pm_s[r, gs], pw_ref[g]) * psc[:, gs])
      ob = jnp.concatenate(obp, axis=1).astype(BF16)
      vals["yb"] = _dot(ob, wb_ref[...])

    def merge_out():
      merged = (ga_s[r, :].astype(F32) * vals["ya"]
                + gb_s[r, :].astype(F32) * vals["yb"]).astype(BF16)
      out_ref[r, :] = x_ref[r, :] + gate1 * _dot(merged, wo_ref[...])

    return [branch_a, branch_b, merge_out]

  row_blocks = lambda r0, n: range(r0, r0 + n, ROWS)

  if has_state:
    assert n_seq == 1 and grid2d and not emit_state
    run(phase1_units(0, tt)[0])
    for dirn in range(2):
      for hd in range(N_HEADS):
        st_s[dirn, hd] = s0_ref[dirn, hd].T

    def body(cj, carry):
      run(scan_units([c for slot in range(CHUNK_SLOTS)
                      for c in chunk_pair(0, cj * CHUNK_SLOTS + slot, False, slot)]))
      return carry

    lax.fori_loop(0, cps // CHUNK_SLOTS, body, 0)
    pool2d()
    for r0 in row_blocks(0, tt):
      run(phase4_units(r0))
  else:
    assert cps <= CHUNK_SLOTS and not grid2d
    p1, n_scan_in = zip(*[phase1_units(sq * seq_len, seq_len) for sq in range(n_seq)])
    n_scan_in = n_scan_in[0]
    blocks = lambda sq: row_blocks(sq * seq_len, seq_len)

    run(p1[0][:n_scan_in])
    for sq in range(n_seq):
      main = scan_units([c for ci in range(cps)
                         for c in chunk_pair(sq * seq_len, ci, ci == 0, ci)])
      if emit_state:
        main.append(functools.partial(emit_states, sq))
      side = p1[sq][n_scan_in:]
      if sq + 1 < n_seq:
        side += p1[sq + 1][:n_scan_in]
      side += [u for r0 in blocks(sq) for u in pool1d_units(r0)]
      if sq > 0:
        side += [u for r0 in blocks(sq - 1) for u in phase4_units(r0)]
      run(interleave(main, side))
    run([u for r0 in blocks(n_seq - 1) for u in phase4_units(r0)])


def _const_spec(shape):
  nd = len(shape)
  return pl.BlockSpec(shape, lambda i, _nd=nd: (0,) * _nd, pipeline_mode=pl.Buffered(1))


def _mix_call(x2d, mod3, mod_row, lbl, n1, win, ng, wa, pw, ps, wb, wo, s0, *, layer, n_seq,
              seq_len, grid2d, emit_state):
  tokens = x2d.shape[0]
  tt = n_seq * seq_len
  assert tokens % tt == 0 and seq_len % (CHUNK * CHUNK_SLOTS) == 0 and tt % ROWS == 0
  steps = tokens // tt
  has_state = s0 is not None
  x_mode = pl.Buffered(1) if steps <= 2 else None
  in_specs = [
      pl.BlockSpec((tt, D_MODEL), lambda i: (i, 0), pipeline_mode=x_mode),
      pl.BlockSpec((None, 1, 6 * D_MODEL), lambda i: (mod_row(i), 0, 0)),
      _const_spec(lbl.shape), _const_spec(n1.shape), _const_spec(win.shape), _const_spec(ng.shape),
      _const_spec(wa.shape), _const_spec(pw.shape), _const_spec(ps.shape), _const_spec(wb.shape),
      _const_spec(wo.shape),
  ]
  args = [x2d, mod3, lbl, n1, win, ng, wa, pw, ps, wb, wo]
  if has_state:
    in_specs.append(pl.BlockSpec((None, None, 2, N_HEADS, HEAD, HEAD),
                                 lambda i: (i, layer, 0, 0, 0, 0)))
    args.append(s0)
  out_specs = [pl.BlockSpec((tt, D_MODEL), lambda i: (i, 0))]
  out_shape = [jax.ShapeDtypeStruct((tokens, D_MODEL), F32)]
  if emit_state:
    out_specs.append(pl.BlockSpec((n_seq, 2, N_HEADS, HEAD, HEAD), lambda i: (i, 0, 0, 0, 0)))
    out_shape.append(jax.ShapeDtypeStruct((tokens // seq_len, 2, N_HEADS, HEAD, HEAD), F32))
  scratch = [
      pltpu.VMEM((tt, D_MODEL), BF16),
      pltpu.VMEM((tt, D_A), F32),
      pltpu.VMEM((tt, 2 * D_A), F32),
      pltpu.VMEM((tt, D_A), BF16),
      pltpu.VMEM((tt, D_A), BF16),
      pltpu.VMEM((tt, D_B), F32),
      pltpu.VMEM((tt, D_MODEL), BF16),
      pltpu.VMEM((tt, D_MODEL), BF16),
      pltpu.VMEM((2 * CHUNK_SLOTS, N_HEADS, CHUNK, HEAD), F32),
      pltpu.VMEM((2 * CHUNK_SLOTS, CHUNK, D_A), F32),
      pltpu.VMEM((2, tt, D_A), F32),
      pltpu.VMEM((tt, D_B), BF16),
      pltpu.VMEM((2, N_HEADS, HEAD, HEAD), F32),
      pltpu.VMEM((2, CHUNK, CHUNK), BF16),
      pltpu.VMEM((2, CHUNK, CHUNK), jnp.int32),
      pltpu.VMEM((N_GROUPS, ROWS, ROWS), BF16),
      pltpu.VMEM((N_GROUPS, ROWS, 1), F32),
      pltpu.VMEM((2 * CHUNK_SLOTS, N_HEADS, 2 * N_UP + 2, N_BLOCKS, HEAD), F32),
  ]
  if grid2d:
    scratch.append(pltpu.VMEM((tt, D_B), F32))
  body = functools.partial(_mix_body, layer=layer, n_seq=n_seq, seq_len=seq_len, grid2d=grid2d,
                           has_state=has_state, emit_state=emit_state)
  return pl.pallas_call(
      body,
      grid=(steps,),
      in_specs=in_specs,
      out_specs=out_specs,
      out_shape=out_shape,
      scratch_shapes=scratch,
      compiler_params=pltpu.CompilerParams(
          dimension_semantics=("arbitrary",), vmem_limit_bytes=VMEM_LIMIT_V7X),
      name="mix_grid2d" if grid2d else "mix_seq1d",
  )(*args)


FFN_COLS = 256


WI_CHUNK = 512
WO_CHUNK = 256
STAGE_WINDOWS = 4


def _stream_cast(w_hbm, w_s, stage, sems, chunks):
  nw = stage.shape[0]

  def copy(k):
    return pltpu.make_async_copy(w_hbm.at[chunks[k]], stage.at[k % nw], sems.at[k % nw])

  for k in range(min(nw - 1, len(chunks))):
    copy(k).start()
  for k in range(len(chunks)):
    if k + nw - 1 < len(chunks):
      copy(k + nw - 1).start()
    copy(k).wait()
    w_s[chunks[k]] = stage[k % nw].astype(BF16)


def _ffn_body(xc_ref, xl_ref, mod_ref, n2_ref, wi_hbm, wo_hbm, fg_ref, oc_ref, ol_ref, h_s, hid_s,
              wi_s, wo_s, sti_s, sto_s, semi, semo, *, ctx_steps):
  tt = xc_ref.shape[0]
  step = pl.program_id(0)
  is_ctx = step < ctx_steps

  @pl.when(step == 0)
  def _():
    _stream_cast(wi_hbm, wi_s, sti_s, semi,
                 [(slice(None), pl.ds(c, WI_CHUNK)) for c in range(0, 2 * D_FF, WI_CHUNK)])
    _stream_cast(wo_hbm, wo_s, sto_s, semo,
                 [(pl.ds(r, WO_CHUNK), slice(None)) for r in range(0, D_FF, WO_CHUNK)])
    oc_ref[...] = jnp.zeros(oc_ref.shape, F32)

  mod = mod_ref[...]
  sh2 = mod[:, 3 * D_MODEL:4 * D_MODEL]
  a2 = n2_ref[...] * (1.0 + mod[:, 4 * D_MODEL:5 * D_MODEL])
  gate2 = mod[:, 5 * D_MODEL:6 * D_MODEL]
  fg = fg_ref[...]

  blocks = [slice(i * ROWS, (i + 1) * ROWS) for i in range(tt // ROWS)]
  pick_x = lambda r: jnp.where(is_ctx, xc_ref[r, :], xl_ref[r, :])
  for r in blocks:
    h_s[r, :] = (_rms_rows(pick_x(r)) * a2 + sh2).astype(BF16)
  for j in range(D_FF // FFN_COLS):
    cg = slice(j * FFN_COLS, (j + 1) * FFN_COLS)
    cu = slice(D_FF + j * FFN_COLS, D_FF + (j + 1) * FFN_COLS)
    for r in blocks:
      gp = _dot(h_s[r, :], wi_s[:, cg])
      up = _dot(h_s[r, :], wi_s[:, cu])
      hid_s[r, cg] = (_silu(gp) * up).astype(BF16)
  for r in blocks:
    y = _dot(hid_s[r, :], wo_s[...])
    new = _rms_rows(pick_x(r) + gate2 * y) * fg
    ol_ref[r, :] = new
    oc_ref[r, :] = jnp.where(is_ctx, new, oc_ref[r, :])


def _ffn_call(x_ctx, x_lat, mod3, n2, wi, wo, fg, *, tt, lat_seq_len):
  ctx_tokens, lat_tokens = x_ctx.shape[0], x_lat.shape[0]
  assert ctx_tokens % tt == 0 and lat_tokens % tt == 0 and lat_seq_len % tt == 0 and tt % ROWS == 0
  assert D_FF % FFN_COLS == 0 and (2 * D_FF) % WI_CHUNK == 0 and D_FF % WO_CHUNK == 0
  nc = ctx_tokens // tt
  steps_per_seq = lat_seq_len // tt
  mod_row = lambda i: jnp.where(i < nc, 0, 1 + (i - nc) // steps_per_seq)
  ctx_blk = lambda i: (jnp.minimum(i, nc - 1), 0)
  lat_blk = lambda i: (jnp.maximum(i - nc, 0), 0)
  body = functools.partial(_ffn_body, ctx_steps=nc)
  return pl.pallas_call(
      body,
      grid=((ctx_tokens + lat_tokens) // tt,),
      in_specs=[
          pl.BlockSpec((tt, D_MODEL), ctx_blk),
          pl.BlockSpec((tt, D_MODEL), lat_blk),
          pl.BlockSpec((None, 1, 6 * D_MODEL), lambda i: (mod_row(i), 0, 0)),
          _const_spec(n2.shape),
          pl.BlockSpec(memory_space=pl.ANY),
          pl.BlockSpec(memory_space=pl.ANY),
          _const_spec(fg.shape),
      ],
      out_specs=[
          pl.BlockSpec((tt, D_MODEL), ctx_blk),
          pl.BlockSpec((tt, D_MODEL), lat_blk),
      ],
      out_shape=[jax.ShapeDtypeStruct((ctx_tokens, D_MODEL), F32),
                 jax.ShapeDtypeStruct((lat_tokens, D_MODEL), F32)],
      scratch_shapes=[
          pltpu.VMEM((tt, D_MODEL), BF16),
          pltpu.VMEM((tt, D_FF), BF16),
          pltpu.VMEM((D_MODEL, 2 * D_FF), BF16),
          pltpu.VMEM((D_FF, D_MODEL), BF16),
          pltpu.VMEM((STAGE_WINDOWS, D_MODEL, WI_CHUNK), F32),
          pltpu.VMEM((STAGE_WINDOWS, WO_CHUNK, D_MODEL), F32),
          pltpu.SemaphoreType.DMA((STAGE_WINDOWS,)),
          pltpu.SemaphoreType.DMA((STAGE_WINDOWS,)),
      ],
      compiler_params=pltpu.CompilerParams(
          dimension_semantics=("arbitrary",), vmem_limit_bytes=VMEM_LIMIT_V7X),
      name="ffn_final",
  )(x_ctx, x_lat, mod3, n2, wi, wo, fg)


def kernel(x_prompt, x_sample, state_hgrn, c, c_ctx, w_ada, b_ada, norm1_g, w_in, hgrn_lb_logits,
           hgrn_norm_g, w_branch_a, pool_w, pool_scale, w_branch_b, w_out, norm2_g, w_ffn_in,
           w_ffn_out, final_g):
  bp, sp, _ = x_prompt.shape
  bs, ss, _ = x_sample.shape
  depth = w_in.shape[0]
  assert bs + 1 <= 8 and ss % GRID_W == 0
  xp = x_prompt.reshape(bp * sp, D_MODEL)
  xs = x_sample.reshape(bs * ss, D_MODEL)
  cvec = jnp.concatenate([c_ctx[None, :], c, jnp.zeros((8 - 1 - bs, D_MODEL), F32)], axis=0)
  row2 = lambda a: a.reshape(1, -1)
  ffn_tt = 512
  ctx_states = []
  for l in range(depth):
    mod, win16, wa16, wb16, wo16, pw16 = _mod_call(
        cvec, w_ada[l], row2(b_ada[l]), [w_in[l], w_branch_a[l], w_branch_b[l], w_out[l]],
        [pool_w[l]])
    mod3 = mod[:1 + bs, None, :]
    mix_w = (hgrn_lb_logits, row2(norm1_g[l]), win16, row2(hgrn_norm_g[l]), wa16, pw16,
             row2(pool_scale[l]), wb16, wo16)
    assert l == depth - 1, "the FFN call also applies the final norm"
    xp, st = _mix_call(xp, mod3, lambda i: 0, *mix_w, None, layer=l, n_seq=2, seq_len=sp,
                       grid2d=False, emit_state=True)
    ctx_states.append(st)
    (xs,) = _mix_call(xs, mod3, lambda i: 1 + i, *mix_w, state_hgrn, layer=l, n_seq=1,
                      seq_len=ss, grid2d=True, emit_state=False)
    xp, xs = _ffn_call(xp, xs, mod3, row2(norm2_g[l]), w_ffn_in[l], w_ffn_out[l], row2(final_g),
                       tt=ffn_tt, lat_seq_len=ss)
  new_state = jnp.stack(ctx_states, axis=1).astype(x_prompt.dtype)
  return (xp.reshape(bp, sp, D_MODEL), xs.reshape(bs, ss, D_MODEL), new_state)
```

```python
import functools

import jax
import jax.numpy as jnp
from jax import lax
from jax.experimental import pallas as pl
from jax.experimental.pallas import tpu as pltpu

F32 = jnp.float32
BF16 = jnp.bfloat16

D_MODEL = 1024
D_A = 512
N_HEADS = 4
HEAD = D_A // N_HEADS
D_B = 512
N_GROUPS = 4
GROUP = D_B // N_GROUPS
POOL_WINDOWS = (2, 4, 8, 16)
GRID_W = 64
D_FF = 2816
IN_COLS = 5 * D_A + D_B + 2 * D_MODEL
EPS = 1e-6

CHUNK = 128
DIAG = 16
CHUNK_SLOTS = 2
PROJ_COLS = 256
N_SCAN_IN = 4
ROWS = 256
EXP_CLAMP = 80.0
VMEM_LIMIT_V7X = 58 * 1024 * 1024


def _sigmoid(x):
  return 0.5 + 0.5 * jnp.tanh(0.5 * x)


def _silu(x):
  h = 0.5 * x
  return h + h * jnp.tanh(h)


def _dot(a, b):
  return jnp.dot(a, b, preferred_element_type=F32)


def _dot_nt(a, b):
  return lax.dot_general(a, b, (((1,), (1,)), ((), ())), preferred_element_type=F32)


def _dot_tn(a, b):
  return lax.dot_general(a, b, (((0,), (0,)), ((), ())), preferred_element_type=F32)


def _split_bf16(x):
  hi = x.astype(BF16)
  lo = (x - hi.astype(F32)).astype(BF16)
  return hi, lo


def _rms_rows(x):
  return x * lax.rsqrt(jnp.mean(x * x, axis=-1, keepdims=True) + EPS)


PREP_STEPS = 8


def _mod_body(c_ref, w_ref, b_ref, *refs):
  n_w = (len(refs) - 1) // 2
  o_ref = refs[n_w]
  cv = c_ref[...]
  s = _silu(cv).astype(BF16)
  o_ref[...] = _dot(s, w_ref[...].astype(BF16)) + b_ref[...]
  for src, dst in zip(refs[:n_w], refs[n_w + 1:]):
    dst[...] = src[...].astype(BF16)


def _mod_call(cvec, w_ada, b_ada, mats, stacks):
  n = w_ada.shape[1]
  assert n % (PREP_STEPS * 128) == 0
  bn = n // PREP_STEPS
  col = lambda j: (0, j)
  in_specs = [pl.BlockSpec((8, D_MODEL), lambda j: (0, 0)),
              pl.BlockSpec((D_MODEL, bn), col),
              pl.BlockSpec((1, bn), col)]
  w_specs, w_shapes = [], []
  for w in mats:
    rows, cols = w.shape
    assert cols % (PREP_STEPS * 128) == 0
    w_specs.append(pl.BlockSpec((rows, cols // PREP_STEPS), col))
    w_shapes.append(jax.ShapeDtypeStruct(w.shape, BF16))
  for w in stacks:
    g, k, m = w.shape
    assert PREP_STEPS % g == 0
    per = PREP_STEPS // g
    w_specs.append(pl.BlockSpec((None, k, m), lambda j, per=per: (j // per, 0, 0)))
    w_shapes.append(jax.ShapeDtypeStruct(w.shape, BF16))
  return pl.pallas_call(
      _mod_body,
      grid=(PREP_STEPS,),
      in_specs=in_specs + w_specs,
      out_specs=[pl.BlockSpec((8, bn), col)] + w_specs,
      out_shape=[jax.ShapeDtypeStruct((8, n), F32)] + w_shapes,
      compiler_params=pltpu.CompilerParams(
          dimension_semantics=("arbitrary",), vmem_limit_bytes=VMEM_LIMIT_V7X),
      name="adaln_mod",
  )(cvec, w_ada, b_ada, *mats, *stacks)


N_BLOCKS = CHUNK // DIAG
N_UP = N_BLOCKS.bit_length() - 1


def _aligned(row):
  return row if isinstance(row, int) else pl.multiple_of(row, CHUNK)


def _query_blocks(level, dirn):
  return [blk for blk in range(N_BLOCKS) if ((blk >> (level - 1)) & 1) != dirn]


def _level_row(level, block, dirn):
  half = DIAG << (level - 1)
  start = (block >> level) * 2 * half
  return start + (half - 1 if dirn == 0 else half)


def _mix_body(*refs, layer, n_seq, seq_len, grid2d, has_state, emit_state):
  (x_ref, mod_ref, lbl_ref, n1_ref, win_ref, ng_ref, wa_ref, pw_ref, ps_ref, wb_ref,
   wo_ref) = refs[:11]
  rest = list(refs[11:])
  s0_ref = rest.pop(0) if has_state else None
  out_ref = rest.pop(0)
  st_ref = rest.pop(0) if emit_state else None
  (h_s, q_s, zf_s, v_s, og_s, zp_s, ga_s, gb_s, b_c, k_c, o_s, pm_s, st_s, tri_s, lev_s,
   band_s, icnt_s, tab_s) = rest[:18]
  cp_s = rest[18] if grid2d else None

  tt = n_seq * seq_len
  cps = seq_len // CHUNK

  ti = lax.broadcasted_iota(jnp.int32, (CHUNK, CHUNK), 0)
  si = lax.broadcasted_iota(jnp.int32, (CHUNK, CHUNK), 1)
  xr = ti ^ si
  lvl = jnp.zeros((CHUNK, CHUNK), jnp.int32)
  for j in range(1, N_UP + 1):
    lvl = jnp.where(xr >= (DIAG << (j - 1)), j, lvl)
  lev_s[0] = jnp.where(ti >= si, lvl, -1)
  lev_s[1] = jnp.where(ti <= si, lvl, -1)
  tri_s[0] = (ti >= si).astype(BF16)
  tri_s[1] = (ti <= si).astype(BF16)

  seg_shift = (GRID_W if grid2d else ROWS).bit_length() - 1
  tr = lax.broadcasted_iota(jnp.int32, (ROWS, ROWS), 0)
  sr = lax.broadcasted_iota(jnp.int32, (ROWS, ROWS), 1)
  same_seg = (tr >> seg_shift) == (sr >> seg_shift)
  for g, w in enumerate(POOL_WINDOWS):
    dlt = sr - tr
    band = (same_seg & (dlt >= -(w // 2)) & (dlt <= w // 2 - 1)).astype(F32)
    band_s[g] = band.astype(BF16)
    icnt_s[g] = 1.0 / jnp.sum(band, axis=1, keepdims=True)

  mod = mod_ref[...]
  sh1 = mod[:, 0:D_MODEL]
  a1 = n1_ref[...] * (1.0 + mod[:, D_MODEL:2 * D_MODEL])
  gate1 = mod[:, 2 * D_MODEL:3 * D_MODEL]
  lg = lbl_ref[...]
  e = jnp.exp(lg - jnp.max(lg, axis=0, keepdims=True))
  lb = jnp.sum(e[:layer + 1], axis=0) / jnp.sum(e, axis=0)

  def run(units):
    for u in units:
      u()

  def interleave(main, side):
    out, taken = [], 0
    for k, u in enumerate(main):
      out.append(u)
      want = ((k + 1) * len(side)) // len(main)
      out.extend(side[taken:want])
      taken = want
    return out

  def phase1_units(r0, n):
    rr = slice(r0, r0 + n)

    def norm():
      for i in range(n // ROWS):
        r = slice(r0 + i * ROWS, r0 + (i + 1) * ROWS)
        h_s[r, :] = (_rms_rows(x_ref[r, :]) * a1 + sh1).astype(BF16)

    def proj(c0, c1):
      return _dot(h_s[rr, :], win_ref[:, c0:c1])

    def put(dst, cols, c0, act, dtype):
      def unit():
        z = proj(c0, c0 + (cols.stop - cols.start))
        dst[rr, cols] = (z if act is None else act(z)).astype(dtype)
      return unit

    layout = [(q_s, 0, D_A, None, F32), (zf_s, 0, D_A, None, F32), (zf_s, D_A, D_A, None, F32),
              (v_s, 0, D_A, None, BF16), (og_s, 0, D_A, _silu, BF16), (zp_s, 0, D_B, None, F32),
              (ga_s, 0, D_MODEL, _sigmoid, BF16), (gb_s, 0, D_MODEL, _sigmoid, BF16)]
    units, n_scan_in, c0 = [norm], None, 0
    for k, (dst, d0, width, act, dtype) in enumerate(layout):
      for j in range(0, width, PROJ_COLS):
        units.append(put(dst, slice(d0 + j, d0 + j + PROJ_COLS), c0 + j, act, dtype))
      c0 += width
      if k + 1 == N_SCAN_IN:
        n_scan_in = len(units)
    return units, n_scan_in

  half_c = 0.5 * (1.0 - lb)
  blk_i = lax.broadcasted_iota(jnp.int32, (N_BLOCKS, HEAD), 0)
  zero_tile = jnp.zeros((DIAG, HEAD), BF16)
  t_eb, t_et = 2 * N_UP, 2 * N_UP + 1

  blk_rows = lambda a, blk: a[blk * DIAG:(blk + 1) * DIAG]

  def stage_gates(base, dirn, slot):
    sd = 2 * slot + dirn
    c1 = half_c[dirn:dirn + 1, :]
    m = c1 * jnp.tanh(0.5 * zf_s[pl.ds(base, CHUNK), dirn * D_A:(dirn + 1) * D_A])
    ghi, glo = _split_bf16(jnp.log((1.0 - c1) + m))
    tri = tri_s[dirn]
    bsum = _dot(tri, ghi) + _dot(tri, glo)
    for hd in range(N_HEADS):
      b_c[sd, hd] = bsum[:, hd * HEAD:(hd + 1) * HEAD]
    k_c[sd] = c1 - m

  def stage_factors(base, dirn, zero_state, slot, hd):
    sd = 2 * slot + dirn
    cs = slice(hd * HEAD, (hd + 1) * HEAD)
    crow = DIAG // 2 if dirn == 0 else DIAG // 2 - 1
    tot_row = CHUNK - 1 if dirn == 0 else 0
    brow = lambda r: b_c[sd, hd, r:r + 1, :]
    tot = brow(tot_row)
    beta = b_c[sd, hd, pl.ds(crow, N_BLOCKS, stride=DIAG), :]
    for l in range(1, N_UP + 1):
      n_par = N_BLOCKS >> l
      ref = brow(_level_row(l, (n_par - 1) << l, dirn))
      for p in range(n_par - 2, -1, -1):
        ref = jnp.where(blk_i < ((p + 1) << l), brow(_level_row(l, p << l, dirn)), ref)
      d = beta - ref
      tab_s[sd, hd, l - 1] = jnp.exp(jnp.minimum(d, 0.0))
      tab_s[sd, hd, N_UP + l - 1] = jnp.exp(jnp.minimum(-d, 0.0))
    tab_s[sd, hd, t_eb] = jnp.exp(beta)
    tab_s[sd, hd, t_et] = jnp.exp(tot - beta)
    trow = lambda t, blk: jnp.broadcast_to(tab_s[sd, hd, t, blk:blk + 1, :], (DIAG, HEAD))

    beta_b = jnp.concatenate(
        [jnp.broadcast_to(brow(blk * DIAG + crow), (DIAG, HEAD)) for blk in range(N_BLOCKS)],
        axis=0)
    de = b_c[sd, hd] - beta_b
    a0 = q_s[pl.ds(base, CHUNK), cs] * jnp.exp(jnp.minimum(de, EXP_CLAMP))
    b0 = k_c[sd, :, cs] * jnp.exp(jnp.minimum(-de, EXP_CLAMP))
    ops = dict(base=base, dirn=dirn, hd=hd, zero_state=zero_state, tot=tot,
               a0=a0.astype(BF16), b0=b0.astype(BF16), lhs={}, rhs={})
    for l in range(1, N_UP + 1):
      qb = _query_blocks(l, dirn)
      ops["lhs"][l] = jnp.concatenate(
          [(blk_rows(a0, blk) * trow(l - 1, blk)).astype(BF16) for blk in qb], axis=0)
      ops["rhs"][l] = jnp.concatenate(
          [zero_tile if blk in qb else (blk_rows(b0, blk) * trow(N_UP + l - 1, blk)).astype(BF16)
           for blk in range(N_BLOCKS)], axis=0)
    ops["kt"] = jnp.concatenate(
        [(blk_rows(b0, blk) * trow(t_et, blk)).astype(BF16) for blk in range(N_BLOCKS)], axis=0)
    if not zero_state:
      ops["qt"] = jnp.concatenate(
          [(blk_rows(a0, blk) * trow(t_eb, blk)).astype(BF16) for blk in range(N_BLOCKS)], axis=0)
    return ops

  def stage_scores(ops):
    dirn = ops["dirn"]
    p_same = _dot_nt(ops["a0"], ops["b0"])
    p_lev = {l: _dot_nt(ops["lhs"][l], ops["rhs"][l]) for l in range(1, N_UP + 1)}
    sc_rows = []
    for blk in range(N_BLOCKS):
      lev = lev_s[dirn, blk * DIAG:(blk + 1) * DIAG, :]
      r = jnp.where(lev == 0, blk_rows(p_same, blk), 0.0)
      for l in range(1, N_UP + 1):
        qb = _query_blocks(l, dirn)
        if blk in qb:
          r = jnp.where(lev == l, blk_rows(p_lev[l], qb.index(blk)), r)
      sc_rows.append(r.astype(BF16))
    ops["sc"] = jnp.concatenate(sc_rows, axis=0)

  def stage_out(ops):
    dirn, hd = ops["dirn"], ops["hd"]
    rows = pl.ds(ops["base"], CHUNK)
    cs = slice(hd * HEAD, (hd + 1) * HEAD)
    vh = v_s[rows, cs]
    o = _dot(ops["sc"], vh)
    if ops["zero_state"]:
      st_s[dirn, hd] = _dot_tn(vh, ops["kt"])
    else:
      st = st_s[dirn, hd]
      o = o + _dot_nt(ops["qt"], st.astype(BF16))
      st_s[dirn, hd] = st * jnp.exp(ops["tot"]) + _dot_tn(vh, ops["kt"])
    o_s[dirn, rows, cs] = o

  def scan_units(chunks):
    group = []
    units = [functools.partial(stage_gates, base, dirn, slot) for base, dirn, _, slot in chunks]
    for base, dirn, zero_state, slot in chunks:
      for hd in range(N_HEADS):
        units.append(lambda a=(base, dirn, zero_state, slot, hd): group.append(stage_factors(*a)))
    n = len(chunks) * N_HEADS
    units += [lambda k=k: stage_scores(group[k]) for k in range(n)]
    units += [lambda k=k: stage_out(group[k]) for k in range(n)]
    return units

  def chunk_pair(seq_base, ci, zero_state, slot):
    return [(_aligned(seq_base + ci * CHUNK), 0, zero_state, slot),
            (_aligned(seq_base + (cps - 1 - ci) * CHUNK), 1, zero_state, slot)]

  def emit_states(sq):
    for dirn in range(2):
      for hd in range(N_HEADS):
        st_ref[sq, dirn, hd] = st_s[dirn, hd].T

  def band_pool(r, g):
    gs = slice(g * GROUP, (g + 1) * GROUP)
    zhi, zlo = _split_bf16(zp_s[r, gs])
    return (_dot(band_s[g], zhi) + _dot(band_s[g], zlo)) * icnt_s[g]

  def pool1d_units(r0):
    r = slice(r0, r0 + ROWS)

    def unit(g):
      gs = slice(g * GROUP, (g + 1) * GROUP)
      pm_s[r, gs] = (band_pool(r, g) - zp_s[r, gs]).astype(BF16)
    return [functools.partial(unit, g) for g in range(N_GROUPS)]

  def pool2d_units():
    n_rows = seq_len // GRID_W

    def cols(r0, g):
      r = slice(r0, r0 + ROWS)
      cp_s[r, g * GROUP:(g + 1) * GROUP] = band_pool(r, g)

    def rows(g, w):
      gs = slice(g * GROUP, (g + 1) * GROUP)
      acc = None
      lo_prev = hi_prev = None
      for rr in range(n_rows):
        lo = max(rr - w // 2, 0)
        hi = min(rr + w // 2 - 1, n_rows - 1)
        if rr == 0:
          acc = cp_s[0:GRID_W, gs]
          for k in range(1, hi + 1):
            acc = acc + cp_s[k * GRID_W:(k + 1) * GRID_W, gs]
        else:
          if hi > hi_prev:
            acc = acc + cp_s[hi * GRID_W:(hi + 1) * GRID_W, gs]
          if lo > lo_prev:
            acc = acc - cp_s[lo_prev * GRID_W:(lo_prev + 1) * GRID_W, gs]
        lo_prev, hi_prev = lo, hi
        rs = slice(rr * GRID_W, (rr + 1) * GRID_W)
        pm_s[rs, gs] = (acc * (1.0 / (hi - lo + 1)) - zp_s[rs, gs]).astype(BF16)

    return ([functools.partial(cols, r0, g) for r0 in range(0, tt, ROWS) for g in range(N_GROUPS)]
            + [functools.partial(rows, g, w) for g, w in enumerate(POOL_WINDOWS)])

  ng = ng_ref[...]
  psc = ps_ref[...]

  def phase4_units(r0):
    r = slice(r0, r0 + ROWS)
    vals = {}

    def branch_a():
      parts = []
      for hd in range(N_HEADS):
        cs = slice(hd * HEAD, (hd + 1) * HEAD)
        parts.append(_rms_rows(o_s[0, r, cs] + o_s[1, r, cs]) * ng[:, cs])
      oa = (jnp.concatenate(parts, axis=1) * og_s[r, :].astype(F32)).astype(BF16)
      vals["ya"] = _dot(oa, wa_ref[...])

    def branch_b():
      obp = []
      for g in range(N_GROUPS):
        gs = slice(g * GROUP, (g + 1) * GROUP)
        obp.append(_dot(pm_s[r, gs], pw_ref[g]) * psc[:, gs])
      ob = jnp.concatenate(obp, axis=1).astype(BF16)
      vals["yb"] = _dot(ob, wb_ref[...])

    def merge_out():
      merged = (ga_s[r, :].astype(F32) * vals["ya"]
                + gb_s[r, :].astype(F32) * vals["yb"]).astype(BF16)
      out_ref[r, :] = x_ref[r, :] + gate1 * _dot(merged, wo_ref[...])

    return [branch_a, branch_b, merge_out]

  row_blocks = lambda r0, n: range(r0, r0 + n, ROWS)

  if has_state:
    assert n_seq == 1 and grid2d and not emit_state
    p1, n_scan_in = phase1_units(0, tt)
    run(p1[:n_scan_in])
    for dirn in range(2):
      for hd in range(N_HEADS):
        st_s[dirn, hd] = s0_ref[dirn, hd].T
    main = [u for cj in range(cps // CHUNK_SLOTS)
            for u in scan_units([c for slot in range(CHUNK_SLOTS)
                                 for c in chunk_pair(0, cj * CHUNK_SLOTS + slot, False, slot)])]
    run(interleave(main, p1[n_scan_in:] + pool2d_units()))
    for r0 in row_blocks(0, tt):
      run(phase4_units(r0))
  else:
    assert cps <= CHUNK_SLOTS and not grid2d
    p1, n_scan_in = zip(*[phase1_units(sq * seq_len, seq_len) for sq in range(n_seq)])
    n_scan_in = n_scan_in[0]
    blocks = lambda sq: row_blocks(sq * seq_len, seq_len)

    run(p1[0][:n_scan_in])
    for sq in range(n_seq):
      main = scan_units([c for ci in range(cps)
                         for c in chunk_pair(sq * seq_len, ci, ci == 0, ci)])
      if emit_state:
        main.append(functools.partial(emit_states, sq))
      side = p1[sq][n_scan_in:]
      if sq + 1 < n_seq:
        side += p1[sq + 1][:n_scan_in]
      side += [u for r0 in blocks(sq) for u in pool1d_units(r0)]
      if sq > 0:
        side += [u for r0 in blocks(sq - 1) for u in phase4_units(r0)]
      run(interleave(main, side))
    run([u for r0 in blocks(n_seq - 1) for u in phase4_units(r0)])


def _const_spec(shape):
  nd = len(shape)
  return pl.BlockSpec(shape, lambda i, _nd=nd: (0,) * _nd, pipeline_mode=pl.Buffered(1))


def _mix_call(x2d, mod3, mod_row, lbl, n1, win, ng, wa, pw, ps, wb, wo, s0, *, layer, n_seq,
              seq_len, grid2d, emit_state):
  tokens = x2d.shape[0]
  tt = n_seq * seq_len
  assert tokens % tt == 0 and seq_len % (CHUNK * CHUNK_SLOTS) == 0 and tt % ROWS == 0
  steps = tokens // tt
  has_state = s0 is not None
  x_mode = pl.Buffered(1) if steps <= 2 else None
  in_specs = [
      pl.BlockSpec((tt, D_MODEL), lambda i: (i, 0), pipeline_mode=x_mode),
      pl.BlockSpec((None, 1, 6 * D_MODEL), lambda i: (mod_row(i), 0, 0)),
      _const_spec(lbl.shape), _const_spec(n1.shape), _const_spec(win.shape), _const_spec(ng.shape),
      _const_spec(wa.shape), _const_spec(pw.shape), _const_spec(ps.shape), _const_spec(wb.shape),
      _const_spec(wo.shape),
  ]
  args = [x2d, mod3, lbl, n1, win, ng, wa, pw, ps, wb, wo]
  if has_state:
    in_specs.append(pl.BlockSpec((None, None, 2, N_HEADS, HEAD, HEAD),
                                 lambda i: (i, layer, 0, 0, 0, 0)))
    args.append(s0)
  out_specs = [pl.BlockSpec((tt, D_MODEL), lambda i: (i, 0))]
  out_shape = [jax.ShapeDtypeStruct((tokens, D_MODEL), F32)]
  if emit_state:
    out_specs.append(pl.BlockSpec((n_seq, 2, N_HEADS, HEAD, HEAD), lambda i: (i, 0, 0, 0, 0)))
    out_shape.append(jax.ShapeDtypeStruct((tokens // seq_len, 2, N_HEADS, HEAD, HEAD), F32))
  scratch = [
      pltpu.VMEM((tt, D_MODEL), BF16),
      pltpu.VMEM((tt, D_A), F32),
      pltpu.VMEM((tt, 2 * D_A), F32),
      pltpu.VMEM((tt, D_A), BF16),
      pltpu.VMEM((tt, D_A), BF16),
      pltpu.VMEM((tt, D_B), F32),
      pltpu.VMEM((tt, D_MODEL), BF16),
      pltpu.VMEM((tt, D_MODEL), BF16),
      pltpu.VMEM((2 * CHUNK_SLOTS, N_HEADS, CHUNK, HEAD), F32),
      pltpu.VMEM((2 * CHUNK_SLOTS, CHUNK, D_A), F32),
      pltpu.VMEM((2, tt, D_A), F32),
      pltpu.VMEM((tt, D_B), BF16),
      pltpu.VMEM((2, N_HEADS, HEAD, HEAD), F32),
      pltpu.VMEM((2, CHUNK, CHUNK), BF16),
      pltpu.VMEM((2, CHUNK, CHUNK), jnp.int32),
      pltpu.VMEM((N_GROUPS, ROWS, ROWS), BF16),
      pltpu.VMEM((N_GROUPS, ROWS, 1), F32),
      pltpu.VMEM((2 * CHUNK_SLOTS, N_HEADS, 2 * N_UP + 2, N_BLOCKS, HEAD), F32),
  ]
  if grid2d:
    scratch.append(pltpu.VMEM((tt, D_B), F32))
  body = functools.partial(_mix_body, layer=layer, n_seq=n_seq, seq_len=seq_len, grid2d=grid2d,
                           has_state=has_state, emit_state=emit_state)
  return pl.pallas_call(
      body,
      grid=(steps,),
      in_specs=in_specs,
      out_specs=out_specs,
      out_shape=out_shape,
      scratch_shapes=scratch,
      compiler_params=pltpu.CompilerParams(
          dimension_semantics=("arbitrary",), vmem_limit_bytes=VMEM_LIMIT_V7X),
      name="mix_grid2d" if grid2d else "mix_seq1d",
  )(*args)


FFN_COLS = 256


WI_CHUNK = 512
WO_CHUNK = 256
STAGE_WINDOWS = 4


def _stream_cast(w_hbm, w_s, stage, sems, chunks):
  nw = stage.shape[0]

  def copy(k):
    return pltpu.make_async_copy(w_hbm.at[chunks[k]], stage.at[k % nw], sems.at[k % nw])

  for k in range(min(nw - 1, len(chunks))):
    copy(k).start()
  for k in range(len(chunks)):
    if k + nw - 1 < len(chunks):
      copy(k + nw - 1).start()
    copy(k).wait()
    w_s[chunks[k]] = stage[k % nw].astype(BF16)


def _ffn_body(xc_ref, xl_ref, mod_ref, n2_ref, wi_hbm, wo_hbm, fg_ref, oc_ref, ol_ref, h_s, hid_s,
              wi_s, wo_s, sti_s, sto_s, semi, semo, *, ctx_steps):
  tt = xc_ref.shape[0]
  step = pl.program_id(0)
  is_ctx = step < ctx_steps

  @pl.when(step == 0)
  def _():
    _stream_cast(wi_hbm, wi_s, sti_s, semi,
                 [(slice(None), pl.ds(c, WI_CHUNK)) for c in range(0, 2 * D_FF, WI_CHUNK)])
    _stream_cast(wo_hbm, wo_s, sto_s, semo,
                 [(pl.ds(r, WO_CHUNK), slice(None)) for r in range(0, D_FF, WO_CHUNK)])
    oc_ref[...] = jnp.zeros(oc_ref.shape, F32)

  mod = mod_ref[...]
  sh2 = mod[:, 3 * D_MODEL:4 * D_MODEL]
  a2 = n2_ref[...] * (1.0 + mod[:, 4 * D_MODEL:5 * D_MODEL])
  gate2 = mod[:, 5 * D_MODEL:6 * D_MODEL]
  fg = fg_ref[...]

  blocks = [slice(i * ROWS, (i + 1) * ROWS) for i in range(tt // ROWS)]
  pick_x = lambda r: jnp.where(is_ctx, xc_ref[r, :], xl_ref[r, :])
  for r in blocks:
    h_s[r, :] = (_rms_rows(pick_x(r)) * a2 + sh2).astype(BF16)
  for j in range(D_FF // FFN_COLS):
    cg = slice(j * FFN_COLS, (j + 1) * FFN_COLS)
    cu = slice(D_FF + j * FFN_COLS, D_FF + (j + 1) * FFN_COLS)
    for r in blocks:
      gp = _dot(h_s[r, :], wi_s[:, cg])
      up = _dot(h_s[r, :], wi_s[:, cu])
      hid_s[r, cg] = (_silu(gp) * up).astype(BF16)
  for r in blocks:
    y = _dot(hid_s[r, :], wo_s[...])
    new = _rms_rows(pick_x(r) + gate2 * y) * fg
    ol_ref[r, :] = new
    oc_ref[r, :] = jnp.where(is_ctx, new, oc_ref[r, :])


def _ffn_call(x_ctx, x_lat, mod3, n2, wi, wo, fg, *, tt, lat_seq_len):
  ctx_tokens, lat_tokens = x_ctx.shape[0], x_lat.shape[0]
  assert ctx_tokens % tt == 0 and lat_tokens % tt == 0 and lat_seq_len % tt == 0 and tt % ROWS == 0
  assert D_FF % FFN_COLS == 0 and (2 * D_FF) % WI_CHUNK == 0 and D_FF % WO_CHUNK == 0
  nc = ctx_tokens // tt
  steps_per_seq = lat_seq_len // tt
  mod_row = lambda i: jnp.where(i < nc, 0, 1 + (i - nc) // steps_per_seq)
  ctx_blk = lambda i: (jnp.minimum(i, nc - 1), 0)
  lat_blk = lambda i: (jnp.maximum(i - nc, 0), 0)
  body = functools.partial(_ffn_body, ctx_steps=nc)
  return pl.pallas_call(
      body,
      grid=((ctx_tokens + lat_tokens) // tt,),
      in_specs=[
          pl.BlockSpec((tt, D_MODEL), ctx_blk),
          pl.BlockSpec((tt, D_MODEL), lat_blk),
          pl.BlockSpec((None, 1, 6 * D_MODEL), lambda i: (mod_row(i), 0, 0)),
          _const_spec(n2.shape),
          pl.BlockSpec(memory_space=pl.ANY),
          pl.BlockSpec(memory_space=pl.ANY),
          _const_spec(fg.shape),
      ],
      out_specs=[
          pl.BlockSpec((tt, D_MODEL), ctx_blk),
          pl.BlockSpec((tt, D_MODEL), lat_blk),
      ],
      out_shape=[jax.ShapeDtypeStruct((ctx_tokens, D_MODEL), F32),
                 jax.ShapeDtypeStruct((lat_tokens, D_MODEL), F32)],
      scratch_shapes=[
          pltpu.VMEM((tt, D_MODEL), BF16),
          pltpu.VMEM((tt, D_FF), BF16),
          pltpu.VMEM((D_MODEL, 2 * D_FF), BF16),
          pltpu.VMEM((D_FF, D_MODEL), BF16),
          pltpu.VMEM((STAGE_WINDOWS, D_MODEL, WI_CHUNK), F32),
          pltpu.VMEM((STAGE_WINDOWS, WO_CHUNK, D_MODEL), F32),
          pltpu.SemaphoreType.DMA((STAGE_WINDOWS,)),
          pltpu.SemaphoreType.DMA((STAGE_WINDOWS,)),
      ],
      compiler_params=pltpu.CompilerParams(
          dimension_semantics=("arbitrary",), vmem_limit_bytes=VMEM_LIMIT_V7X),
      name="ffn_final",
  )(x_ctx, x_lat, mod3, n2, wi, wo, fg)


def kernel(x_prompt, x_sample, state_hgrn, c, c_ctx, w_ada, b_ada, norm1_g, w_in, hgrn_lb_logits,
           hgrn_norm_g, w_branch_a, pool_w, pool_scale, w_branch_b, w_out, norm2_g, w_ffn_in,
           w_ffn_out, final_g):
  bp, sp, _ = x_prompt.shape
  bs, ss, _ = x_sample.shape
  depth = w_in.shape[0]
  assert bs + 1 <= 8 and ss % GRID_W == 0
  xp = x_prompt.reshape(bp * sp, D_MODEL)
  xs = x_sample.reshape(bs * ss, D_MODEL)
  cvec = jnp.concatenate([c_ctx[None, :], c, jnp.zeros((8 - 1 - bs, D_MODEL), F32)], axis=0)
  row2 = lambda a: a.reshape(1, -1)
  ffn_tt = 512
  ctx_states = []
  for l in range(depth):
    mod, win16, wa16, wb16, wo16, pw16 = _mod_call(
        cvec, w_ada[l], row2(b_ada[l]), [w_in[l], w_branch_a[l], w_branch_b[l], w_out[l]],
        [pool_w[l]])
    mod3 = mod[:1 + bs, None, :]
    mix_w = (hgrn_lb_logits, row2(norm1_g[l]), win16, row2(hgrn_norm_g[l]), wa16, pw16,
             row2(pool_scale[l]), wb16, wo16)
    assert l == depth - 1, "the FFN call also applies the final norm"
    xp, st = _mix_call(xp, mod3, lambda i: 0, *mix_w, None, layer=l, n_seq=2, seq_len=sp,
                       grid2d=False, emit_state=True)
    ctx_states.append(st)
    (xs,) = _mix_call(xs, mod3, lambda i: 1 + i, *mix_w, state_hgrn, layer=l, n_seq=1,
                      seq_len=ss, grid2d=True, emit_state=False)
    xp, xs = _ffn_call(xp, xs, mod3, row2(norm2_g[l]), w_ffn_in[l], w_ffn_out[l], row2(final_g),
                       tt=ffn_tt, lat_seq_len=ss)
  new_state = jnp.stack(ctx_states, axis=1).astype(x_prompt.dtype)
  return (xp.reshape(bp, sp, D_MODEL), xs.reshape(bs, ss, D_MODEL), new_state)
```

```python
import functools

import jax
import jax.numpy as jnp
from jax import lax
from jax.experimental import pallas as pl
from jax.experimental.pallas import tpu as pltpu

F32 = jnp.float32
BF16 = jnp.bfloat16

D_MODEL = 1024
D_A = 512
N_HEADS = 4
HEAD = D_A // N_HEADS
D_B = 512
N_GROUPS = 4
GROUP = D_B // N_GROUPS
POOL_WINDOWS = (2, 4, 8, 16)
GRID_W = 64
D_FF = 2816
IN_COLS = 5 * D_A + D_B + 2 * D_MODEL
EPS = 1e-6

CHUNK = 128
DIAG = 16
CHUNK_SLOTS = 2
PROJ_COLS = 256
N_SCAN_IN = 4
ROWS = 256
EXP_CLAMP = 80.0
VMEM_LIMIT_V7X = 58 * 1024 * 1024


def _sigmoid(x):
  return 0.5 + 0.5 * jnp.tanh(0.5 * x)


def _silu(x):
  h = 0.5 * x
  return h + h * jnp.tanh(h)


def _dot(a, b):
  return jnp.dot(a, b, preferred_element_type=F32)


def _dot_nt(a, b):
  return jnp.dot(a, jnp.transpose(b), preferred_element_type=F32)


def _dot_tn(a, b):
  return lax.dot_general(a, b, (((0,), (0,)), ((), ())), preferred_element_type=F32)


def _split_bf16(x):
  hi = x.astype(BF16)
  lo = (x - hi.astype(F32)).astype(BF16)
  return hi, lo


def _rms_rows(x):
  return x * lax.rsqrt(jnp.mean(x * x, axis=-1, keepdims=True) + EPS)


PREP_STEPS = 8


def _mod_body(c_ref, w_ref, b_ref, *refs):
  n_w = (len(refs) - 1) // 2
  o_ref = refs[n_w]
  cv = c_ref[...]
  s = _silu(cv).astype(BF16)
  o_ref[...] = _dot(s, w_ref[...].astype(BF16)) + b_ref[...]
  for src, dst in zip(refs[:n_w], refs[n_w + 1:]):
    dst[...] = src[...].astype(BF16)


def _mod_call(cvec, w_ada, b_ada, mats, stacks):
  n = w_ada.shape[1]
  assert n % (PREP_STEPS * 128) == 0
  bn = n // PREP_STEPS
  col = lambda j: (0, j)
  in_specs = [pl.BlockSpec((8, D_MODEL), lambda j: (0, 0)),
              pl.BlockSpec((D_MODEL, bn), col),
              pl.BlockSpec((1, bn), col)]
  w_specs, w_shapes = [], []
  for w in mats:
    rows, cols = w.shape
    assert cols % (PREP_STEPS * 128) == 0
    w_specs.append(pl.BlockSpec((rows, cols // PREP_STEPS), col))
    w_shapes.append(jax.ShapeDtypeStruct(w.shape, BF16))
  for w in stacks:
    g, k, m = w.shape
    assert PREP_STEPS % g == 0
    per = PREP_STEPS // g
    w_specs.append(pl.BlockSpec((None, k, m), lambda j, per=per: (j // per, 0, 0)))
    w_shapes.append(jax.ShapeDtypeStruct(w.shape, BF16))
  return pl.pallas_call(
      _mod_body,
      grid=(PREP_STEPS,),
      in_specs=in_specs + w_specs,
      out_specs=[pl.BlockSpec((8, bn), col)] + w_specs,
      out_shape=[jax.ShapeDtypeStruct((8, n), F32)] + w_shapes,
      compiler_params=pltpu.CompilerParams(
          dimension_semantics=("arbitrary",), vmem_limit_bytes=VMEM_LIMIT_V7X),
      name="adaln_mod",
  )(cvec, w_ada, b_ada, *mats, *stacks)


N_BLOCKS = CHUNK // DIAG
N_UP = N_BLOCKS.bit_length() - 1


def _aligned(row):
  return row if isinstance(row, int) else pl.multiple_of(row, CHUNK)


def _query_blocks(level, dirn):
  return [blk for blk in range(N_BLOCKS) if ((blk >> (level - 1)) & 1) != dirn]


def _level_row(level, block, dirn):
  half = DIAG << (level - 1)
  start = (block >> level) * 2 * half
  return start + (half - 1 if dirn == 0 else half)


def _mix_body(*refs, layer, n_seq, seq_len, grid2d, has_state, emit_state):
  (x_ref, mod_ref, lbl_ref, n1_ref, win_ref, ng_ref, wa_ref, pw_ref, ps_ref, wb_ref,
   wo_ref) = refs[:11]
  rest = list(refs[11:])
  s0_ref = rest.pop(0) if has_state else None
  out_ref = rest.pop(0)
  st_ref = rest.pop(0) if emit_state else None
  (h_s, q_s, zf_s, v_s, og_s, zp_s, ga_s, gb_s, b_c, k_c, o_s, pm_s, st_s, tri_s, lev_s,
   band_s, icnt_s, tab_s, xt_s) = rest[:19]
  cp_s = rest[19] if grid2d else None

  tt = n_seq * seq_len
  cps = seq_len // CHUNK

  ti = lax.broadcasted_iota(jnp.int32, (CHUNK, CHUNK), 0)
  si = lax.broadcasted_iota(jnp.int32, (CHUNK, CHUNK), 1)
  xr = ti ^ si
  lvl = jnp.zeros((CHUNK, CHUNK), jnp.int32)
  for j in range(1, N_UP + 1):
    lvl = jnp.where(xr >= (DIAG << (j - 1)), j, lvl)
  lev_s[0] = jnp.where(ti >= si, lvl, -1)
  lev_s[1] = jnp.where(ti <= si, lvl, -1)
  tri_s[0] = (ti >= si).astype(BF16)
  tri_s[1] = (ti <= si).astype(BF16)

  seg_shift = (GRID_W if grid2d else ROWS).bit_length() - 1
  tr = lax.broadcasted_iota(jnp.int32, (ROWS, ROWS), 0)
  sr = lax.broadcasted_iota(jnp.int32, (ROWS, ROWS), 1)
  same_seg = (tr >> seg_shift) == (sr >> seg_shift)
  for g, w in enumerate(POOL_WINDOWS):
    dlt = sr - tr
    band = (same_seg & (dlt >= -(w // 2)) & (dlt <= w // 2 - 1)).astype(F32)
    band_s[g] = band.astype(BF16)
    icnt_s[g] = 1.0 / jnp.sum(band, axis=1, keepdims=True)

  mod = mod_ref[...]
  sh1 = mod[:, 0:D_MODEL]
  a1 = n1_ref[...] * (1.0 + mod[:, D_MODEL:2 * D_MODEL])
  gate1 = mod[:, 2 * D_MODEL:3 * D_MODEL]
  lg = lbl_ref[...]
  e = jnp.exp(lg - jnp.max(lg, axis=0, keepdims=True))
  lb = jnp.sum(e[:layer + 1], axis=0) / jnp.sum(e, axis=0)

  def run(units):
    for u in units:
      u()

  def interleave(main, side):
    out, taken = [], 0
    for k, u in enumerate(main):
      out.append(u)
      want = ((k + 1) * len(side)) // len(main)
      out.extend(side[taken:want])
      taken = want
    return out

  def phase1_units(r0, n):
    rr = slice(r0, r0 + n)

    def norm():
      for i in range(n // ROWS):
        r = slice(r0 + i * ROWS, r0 + (i + 1) * ROWS)
        h_s[r, :] = (_rms_rows(x_ref[r, :]) * a1 + sh1).astype(BF16)

    def proj(c0, c1):
      return _dot(h_s[rr, :], win_ref[:, c0:c1])

    def put(dst, cols, c0, act, dtype):
      def unit():
        z = proj(c0, c0 + (cols.stop - cols.start))
        dst[rr, cols] = (z if act is None else act(z)).astype(dtype)
      return unit

    layout = [(q_s, 0, D_A, None, F32), (zf_s, 0, D_A, None, F32), (zf_s, D_A, D_A, None, F32),
              (v_s, 0, D_A, None, BF16), (og_s, 0, D_A, _silu, BF16), (zp_s, 0, D_B, None, F32),
              (ga_s, 0, D_MODEL, _sigmoid, BF16), (gb_s, 0, D_MODEL, _sigmoid, BF16)]
    units, n_scan_in, c0 = [norm], None, 0
    for k, (dst, d0, width, act, dtype) in enumerate(layout):
      for j in range(0, width, PROJ_COLS):
        units.append(put(dst, slice(d0 + j, d0 + j + PROJ_COLS), c0 + j, act, dtype))
      c0 += width
      if k + 1 == N_SCAN_IN:
        n_scan_in = len(units)
    return units, n_scan_in

  half_c = 0.5 * (1.0 - lb)
  blk_i = lax.broadcasted_iota(jnp.int32, (N_BLOCKS, HEAD), 0)
  zero_tile = jnp.zeros((DIAG, HEAD), BF16)
  t_eb, t_et = 2 * N_UP, 2 * N_UP + 1

  blk_rows = lambda a, blk: a[blk * DIAG:(blk + 1) * DIAG]

  def stage_gates(base, dirn, slot):
    sd = 2 * slot + dirn
    c1 = half_c[dirn:dirn + 1, :]
    m = c1 * jnp.tanh(0.5 * zf_s[pl.ds(base, CHUNK), dirn * D_A:(dirn + 1) * D_A])
    ghi, glo = _split_bf16(jnp.log((1.0 - c1) + m))
    tri = tri_s[dirn]
    bsum = _dot(tri, ghi) + _dot(tri, glo)
    for hd in range(N_HEADS):
      b_c[sd, hd] = bsum[:, hd * HEAD:(hd + 1) * HEAD]
    k_c[sd] = c1 - m

  def stage_factors(base, dirn, zero_state, slot, hd):
    sd = 2 * slot + dirn
    cs = slice(hd * HEAD, (hd + 1) * HEAD)
    crow = DIAG // 2 if dirn == 0 else DIAG // 2 - 1
    tot_row = CHUNK - 1 if dirn == 0 else 0
    brow = lambda r: b_c[sd, hd, r:r + 1, :]
    tot = brow(tot_row)
    beta = b_c[sd, hd, pl.ds(crow, N_BLOCKS, stride=DIAG), :]
    for l in range(1, N_UP + 1):
      n_par = N_BLOCKS >> l
      ref = brow(_level_row(l, (n_par - 1) << l, dirn))
      for p in range(n_par - 2, -1, -1):
        ref = jnp.where(blk_i < ((p + 1) << l), brow(_level_row(l, p << l, dirn)), ref)
      d = beta - ref
      tab_s[sd, hd, l - 1] = jnp.exp(jnp.minimum(d, 0.0))
      tab_s[sd, hd, N_UP + l - 1] = jnp.exp(jnp.minimum(-d, 0.0))
    tab_s[sd, hd, t_eb] = jnp.exp(beta)
    tab_s[sd, hd, t_et] = jnp.exp(tot - beta)
    trow = lambda t, blk: jnp.broadcast_to(tab_s[sd, hd, t, blk:blk + 1, :], (DIAG, HEAD))

    beta_b = jnp.concatenate(
        [jnp.broadcast_to(brow(blk * DIAG + crow), (DIAG, HEAD)) for blk in range(N_BLOCKS)],
        axis=0)
    de = b_c[sd, hd] - beta_b
    a0 = q_s[pl.ds(base, CHUNK), cs] * jnp.exp(jnp.minimum(de, EXP_CLAMP))
    b0 = k_c[sd, :, cs] * jnp.exp(jnp.minimum(-de, EXP_CLAMP))
    ch = sd * N_HEADS + hd
    ops = dict(base=base, dirn=dirn, hd=hd, zero_state=zero_state, tot=tot, ch=ch,
               a0=a0.astype(BF16), lhs={})
    xt_s[ch, 0] = jnp.transpose(b0.astype(BF16))
    for l in range(1, N_UP + 1):
      qb = _query_blocks(l, dirn)
      ops["lhs"][l] = jnp.concatenate(
          [(blk_rows(a0, blk) * trow(l - 1, blk)).astype(BF16) for blk in qb], axis=0)
      xt_s[ch, l] = jnp.transpose(jnp.concatenate(
          [zero_tile if blk in qb else (blk_rows(b0, blk) * trow(N_UP + l - 1, blk)).astype(BF16)
           for blk in range(N_BLOCKS)], axis=0))
    ops["kt"] = jnp.concatenate(
        [(blk_rows(b0, blk) * trow(t_et, blk)).astype(BF16) for blk in range(N_BLOCKS)], axis=0)
    if not zero_state:
      ops["qt"] = jnp.concatenate(
          [(blk_rows(a0, blk) * trow(t_eb, blk)).astype(BF16) for blk in range(N_BLOCKS)], axis=0)
    return ops

  def stage_scores(ops):
    dirn = ops["dirn"]
    ch = ops["ch"]
    p_same = _dot(ops["a0"], xt_s[ch, 0])
    p_lev = {l: _dot(ops["lhs"][l], xt_s[ch, l]) for l in range(1, N_UP + 1)}
    sc_rows = []
    for blk in range(N_BLOCKS):
      lev = lev_s[dirn, blk * DIAG:(blk + 1) * DIAG, :]
      r = jnp.where(lev == 0, blk_rows(p_same, blk), 0.0)
      for l in range(1, N_UP + 1):
        qb = _query_blocks(l, dirn)
        if blk in qb:
          r = jnp.where(lev == l, blk_rows(p_lev[l], qb.index(blk)), r)
      sc_rows.append(r.astype(BF16))
    ops["sc"] = jnp.concatenate(sc_rows, axis=0)

  def stage_out(ops):
    dirn, hd = ops["dirn"], ops["hd"]
    rows = pl.ds(ops["base"], CHUNK)
    cs = slice(hd * HEAD, (hd + 1) * HEAD)
    vh = v_s[rows, cs]
    o = _dot(ops["sc"], vh)
    if ops["zero_state"]:
      st_s[dirn, hd] = _dot_tn(vh, ops["kt"])
    else:
      st = st_s[dirn, hd]
      xt_s[ops["ch"], N_UP + 1] = jnp.transpose(st).astype(BF16)
      o = o + _dot(ops["qt"], xt_s[ops["ch"], N_UP + 1])
      st_s[dirn, hd] = st * jnp.exp(ops["tot"]) + _dot_tn(vh, ops["kt"])
    o_s[dirn, rows, cs] = o

  def scan_units(chunks):
    group = []
    units = [functools.partial(stage_gates, base, dirn, slot) for base, dirn, _, slot in chunks]
    for base, dirn, zero_state, slot in chunks:
      for hd in range(N_HEADS):
        units.append(lambda a=(base, dirn, zero_state, slot, hd): group.append(stage_factors(*a)))
    n = len(chunks) * N_HEADS
    units += [lambda k=k: stage_scores(group[k]) for k in range(n)]
    units += [lambda k=k: stage_out(group[k]) for k in range(n)]
    return units

  def chunk_pair(seq_base, ci, zero_state, slot):
    return [(_aligned(seq_base + ci * CHUNK), 0, zero_state, slot),
            (_aligned(seq_base + (cps - 1 - ci) * CHUNK), 1, zero_state, slot)]

  def emit_states(sq):
    for dirn in range(2):
      for hd in range(N_HEADS):
        st_ref[sq, dirn, hd] = st_s[dirn, hd].T

  def band_pool(r, g):
    gs = slice(g * GROUP, (g + 1) * GROUP)
    zhi, zlo = _split_bf16(zp_s[r, gs])
    return (_dot(band_s[g], zhi) + _dot(band_s[g], zlo)) * icnt_s[g]

  def pool1d_units(r0):
    r = slice(r0, r0 + ROWS)

    def unit(g):
      gs = slice(g * GROUP, (g + 1) * GROUP)
      pm_s[r, gs] = (band_pool(r, g) - zp_s[r, gs]).astype(BF16)
    return [functools.partial(unit, g) for g in range(N_GROUPS)]

  def pool2d_units():
    n_rows = seq_len // GRID_W

    def cols(r0, g):
      r = slice(r0, r0 + ROWS)
      cp_s[r, g * GROUP:(g + 1) * GROUP] = band_pool(r, g)

    def rows(g, w):
      gs = slice(g * GROUP, (g + 1) * GROUP)
      acc = None
      lo_prev = hi_prev = None
      for rr in range(n_rows):
        lo = max(rr - w // 2, 0)
        hi = min(rr + w // 2 - 1, n_rows - 1)
        if rr == 0:
          acc = cp_s[0:GRID_W, gs]
          for k in range(1, hi + 1):
            acc = acc + cp_s[k * GRID_W:(k + 1) * GRID_W, gs]
        else:
          if hi > hi_prev:
            acc = acc + cp_s[hi * GRID_W:(hi + 1) * GRID_W, gs]
          if lo > lo_prev:
            acc = acc - cp_s[lo_prev * GRID_W:(lo_prev + 1) * GRID_W, gs]
        lo_prev, hi_prev = lo, hi
        rs = slice(rr * GRID_W, (rr + 1) * GRID_W)
        pm_s[rs, gs] = (acc * (1.0 / (hi - lo + 1)) - zp_s[rs, gs]).astype(BF16)

    return ([functools.partial(cols, r0, g) for r0 in range(0, tt, ROWS) for g in range(N_GROUPS)]
            + [functools.partial(rows, g, w) for g, w in enumerate(POOL_WINDOWS)])

  ng = ng_ref[...]
  psc = ps_ref[...]

  def phase4_units(r0):
    r = slice(r0, r0 + ROWS)
    vals = {}

    def branch_a():
      parts = []
      for hd in range(N_HEADS):
        cs = slice(hd * HEAD, (hd + 1) * HEAD)
        parts.append(_rms_rows(o_s[0, r, cs] + o_s[1, r, cs]) * ng[:, cs])
      oa = (jnp.concatenate(parts, axis=1) * og_s[r, :].astype(F32)).astype(BF16)
      vals["ya"] = _dot(oa, wa_ref[...])

    def branch_b():
      obp = []
      for g in range(N_GROUPS):
        gs = slice(g * GROUP, (g + 1) * GROUP)
        obp.append(_dot(pm_s[r, gs], pw_ref[g]) * psc[:, gs])
      ob = jnp.concatenate(obp, axis=1).astype(BF16)
      vals["yb"] = _dot(ob, wb_ref[...])

    def merge_out():
      merged = (ga_s[r, :].astype(F32) * vals["ya"]
                + gb_s[r, :].astype(F32) * vals["yb"]).astype(BF16)
      out_ref[r, :] = x_ref[r, :] + gate1 * _dot(merged, wo_ref[...])

    return [branch_a, branch_b, merge_out]

  row_blocks = lambda r0, n: range(r0, r0 + n, ROWS)

  if has_state:
    assert n_seq == 1 and grid2d and not emit_state
    p1, n_scan_in = phase1_units(0, tt)
    run(p1[:n_scan_in])
    for dirn in range(2):
      for hd in range(N_HEADS):
        st_s[dirn, hd] = s0_ref[dirn, hd].T
    main = [u for cj in range(cps // CHUNK_SLOTS)
            for u in scan_units([c for slot in range(CHUNK_SLOTS)
                                 for c in chunk_pair(0, cj * CHUNK_SLOTS + slot, False, slot)])]
    run(interleave(main, p1[n_scan_in:] + pool2d_units()))
    for r0 in row_blocks(0, tt):
      run(phase4_units(r0))
  else:
    assert cps <= CHUNK_SLOTS and not grid2d
    p1, n_scan_in = phase1_units(0, tt)
    blocks = lambda sq: row_blocks(sq * seq_len, seq_len)

    run(p1[:n_scan_in])
    for sq in range(n_seq):
      main = scan_units([c for ci in range(cps)
                         for c in chunk_pair(sq * seq_len, ci, ci == 0, ci)])
      if emit_state:
        main.append(functools.partial(emit_states, sq))
      side = []
      if sq == 0:
        side += p1[n_scan_in:]
        side += [u for s2 in range(n_seq) for r0 in blocks(s2) for u in pool1d_units(r0)]
      else:
        side += [u for r0 in blocks(sq - 1) for u in phase4_units(r0)]
      run(interleave(main, side))
    run([u for r0 in blocks(n_seq - 1) for u in phase4_units(r0)])


def _const_spec(shape):
  nd = len(shape)
  return pl.BlockSpec(shape, lambda i, _nd=nd: (0,) * _nd, pipeline_mode=pl.Buffered(1))


def _mix_call(x2d, mod3, mod_row, lbl, n1, win, ng, wa, pw, ps, wb, wo, s0, *, layer, n_seq,
              seq_len, grid2d, emit_state):
  tokens = x2d.shape[0]
  tt = n_seq * seq_len
  assert tokens % tt == 0 and seq_len % (CHUNK * CHUNK_SLOTS) == 0 and tt % ROWS == 0
  steps = tokens // tt
  has_state = s0 is not None
  x_mode = pl.Buffered(1) if steps <= 2 else None
  in_specs = [
      pl.BlockSpec((tt, D_MODEL), lambda i: (i, 0), pipeline_mode=x_mode),
      pl.BlockSpec((None, 1, 6 * D_MODEL), lambda i: (mod_row(i), 0, 0)),
      _const_spec(lbl.shape), _const_spec(n1.shape), _const_spec(win.shape), _const_spec(ng.shape),
      _const_spec(wa.shape), _const_spec(pw.shape), _const_spec(ps.shape), _const_spec(wb.shape),
      _const_spec(wo.shape),
  ]
  args = [x2d, mod3, lbl, n1, win, ng, wa, pw, ps, wb, wo]
  if has_state:
    in_specs.append(pl.BlockSpec((None, None, 2, N_HEADS, HEAD, HEAD),
                                 lambda i: (i, layer, 0, 0, 0, 0)))
    args.append(s0)
  out_specs = [pl.BlockSpec((tt, D_MODEL), lambda i: (i, 0))]
  out_shape = [jax.ShapeDtypeStruct((tokens, D_MODEL), F32)]
  if emit_state:
    out_specs.append(pl.BlockSpec((n_seq, 2, N_HEADS, HEAD, HEAD), lambda i: (i, 0, 0, 0, 0)))
    out_shape.append(jax.ShapeDtypeStruct((tokens // seq_len, 2, N_HEADS, HEAD, HEAD), F32))
  scratch = [
      pltpu.VMEM((tt, D_MODEL), BF16),
      pltpu.VMEM((tt, D_A), F32),
      pltpu.VMEM((tt, 2 * D_A), F32),
      pltpu.VMEM((tt, D_A), BF16),
      pltpu.VMEM((tt, D_A), BF16),
      pltpu.VMEM((tt, D_B), F32),
      pltpu.VMEM((tt, D_MODEL), BF16),
      pltpu.VMEM((tt, D_MODEL), BF16),
      pltpu.VMEM((2 * CHUNK_SLOTS, N_HEADS, CHUNK, HEAD), F32),
      pltpu.VMEM((2 * CHUNK_SLOTS, CHUNK, D_A), F32),
      pltpu.VMEM((2, tt, D_A), F32),
      pltpu.VMEM((tt, D_B), BF16),
      pltpu.VMEM((2, N_HEADS, HEAD, HEAD), F32),
      pltpu.VMEM((2, CHUNK, CHUNK), BF16),
      pltpu.VMEM((2, CHUNK, CHUNK), jnp.int32),
      pltpu.VMEM((N_GROUPS, ROWS, ROWS), BF16),
      pltpu.VMEM((N_GROUPS, ROWS, 1), F32),
      pltpu.VMEM((2 * CHUNK_SLOTS, N_HEADS, 2 * N_UP + 2, N_BLOCKS, HEAD), F32),
      pltpu.VMEM((2 * CHUNK_SLOTS * N_HEADS, N_UP + 2, HEAD, CHUNK), BF16),
  ]
  if grid2d:
    scratch.append(pltpu.VMEM((tt, D_B), F32))
  body = functools.partial(_mix_body, layer=layer, n_seq=n_seq, seq_len=seq_len, grid2d=grid2d,
                           has_state=has_state, emit_state=emit_state)
  return pl.pallas_call(
      body,
      grid=(steps,),
      in_specs=in_specs,
      out_specs=out_specs,
      out_shape=out_shape,
      scratch_shapes=scratch,
      compiler_params=pltpu.CompilerParams(
          dimension_semantics=("arbitrary",), vmem_limit_bytes=VMEM_LIMIT_V7X),
      name="mix_grid2d" if grid2d else "mix_seq1d",
  )(*args)


FFN_COLS = 256
FFN_ROWS = 256


WI_CHUNK = 512
WO_CHUNK = 256
STAGE_WINDOWS = 4


def _stream_cast(w_hbm, w_s, stage, sems, chunks):
  nw = stage.shape[0]

  def copy(k):
    return pltpu.make_async_copy(w_hbm.at[chunks[k]], stage.at[k % nw], sems.at[k % nw])

  for k in range(min(nw - 1, len(chunks))):
    copy(k).start()
  for k in range(len(chunks)):
    if k + nw - 1 < len(chunks):
      copy(k + nw - 1).start()
    copy(k).wait()
    w_s[chunks[k]] = stage[k % nw].astype(BF16)


def _ffn_body(xc_ref, xl_ref, mod_ref, n2_ref, wi_hbm, wo_hbm, fg_ref, oc_ref, ol_ref, h_s, hid_s,
              wi_s, wo_s, sti_s, sto_s, semi, semo, *, ctx_steps):
  tt = xc_ref.shape[0]
  step = pl.program_id(0)
  is_ctx = step < ctx_steps

  @pl.when(step == 0)
  def _():
    _stream_cast(wi_hbm, wi_s, sti_s, semi,
                 [(slice(None), pl.ds(c, WI_CHUNK)) for c in range(0, 2 * D_FF, WI_CHUNK)])
    _stream_cast(wo_hbm, wo_s, sto_s, semo,
                 [(pl.ds(r, WO_CHUNK), slice(None)) for r in range(0, D_FF, WO_CHUNK)])
    oc_ref[...] = jnp.zeros(oc_ref.shape, F32)

  mod = mod_ref[...]
  sh2 = mod[:, 3 * D_MODEL:4 * D_MODEL]
  a2 = n2_ref[...] * (1.0 + mod[:, 4 * D_MODEL:5 * D_MODEL])
  gate2 = mod[:, 5 * D_MODEL:6 * D_MODEL]
  fg = fg_ref[...]

  blocks = [slice(i * FFN_ROWS, (i + 1) * FFN_ROWS) for i in range(tt // FFN_ROWS)]
  pick_x = lambda r: jnp.where(is_ctx, xc_ref[r, :], xl_ref[r, :])
  for r in blocks:
    h_s[r, :] = (_rms_rows(pick_x(r)) * a2 + sh2).astype(BF16)
  for j in range(D_FF // FFN_COLS):
    cg = slice(j * FFN_COLS, (j + 1) * FFN_COLS)
    cu = slice(D_FF + j * FFN_COLS, D_FF + (j + 1) * FFN_COLS)
    for r in blocks:
      gp = _dot(h_s[r, :], wi_s[:, cg])
      up = _dot(h_s[r, :], wi_s[:, cu])
      hid_s[r, cg] = (_silu(gp) * up).astype(BF16)
  for r in blocks:
    y = _dot(hid_s[r, :], wo_s[...])
    new = _rms_rows(pick_x(r) + gate2 * y) * fg
    ol_ref[r, :] = new
    oc_ref[r, :] = jnp.where(is_ctx, new, oc_ref[r, :])


def _ffn_call(x_ctx, x_lat, mod3, n2, wi, wo, fg, *, tt, lat_seq_len):
  ctx_tokens, lat_tokens = x_ctx.shape[0], x_lat.shape[0]
  assert ctx_tokens % tt == 0 and lat_tokens % tt == 0 and lat_seq_len % tt == 0 and tt % ROWS == 0
  assert D_FF % FFN_COLS == 0 and (2 * D_FF) % WI_CHUNK == 0 and D_FF % WO_CHUNK == 0
  nc = ctx_tokens // tt
  steps_per_seq = lat_seq_len // tt
  mod_row = lambda i: jnp.where(i < nc, 0, 1 + (i - nc) // steps_per_seq)
  ctx_blk = lambda i: (jnp.minimum(i, nc - 1), 0)
  lat_blk = lambda i: (jnp.maximum(i - nc, 0), 0)
  body = functools.partial(_ffn_body, ctx_steps=nc)
  return pl.pallas_call(
      body,
      grid=((ctx_tokens + lat_tokens) // tt,),
      in_specs=[
          pl.BlockSpec((tt, D_MODEL), ctx_blk),
          pl.BlockSpec((tt, D_MODEL), lat_blk),
          pl.BlockSpec((None, 1, 6 * D_MODEL), lambda i: (mod_row(i), 0, 0)),
          _const_spec(n2.shape),
          pl.BlockSpec(memory_space=pl.ANY),
          pl.BlockSpec(memory_space=pl.ANY),
          _const_spec(fg.shape),
      ],
      out_specs=[
          pl.BlockSpec((tt, D_MODEL), ctx_blk),
          pl.BlockSpec((tt, D_MODEL), lat_blk),
      ],
      out_shape=[jax.ShapeDtypeStruct((ctx_tokens, D_MODEL), F32),
                 jax.ShapeDtypeStruct((lat_tokens, D_MODEL), F32)],
      scratch_shapes=[
          pltpu.VMEM((tt, D_MODEL), BF16),
          pltpu.VMEM((tt, D_FF), BF16),
          pltpu.VMEM((D_MODEL, 2 * D_FF), BF16),
          pltpu.VMEM((D_FF, D_MODEL), BF16),
          pltpu.VMEM((STAGE_WINDOWS, D_MODEL, WI_CHUNK), F32),
          pltpu.VMEM((STAGE_WINDOWS, WO_CHUNK, D_MODEL), F32),
          pltpu.SemaphoreType.DMA((STAGE_WINDOWS,)),
          pltpu.SemaphoreType.DMA((STAGE_WINDOWS,)),
      ],
      compiler_params=pltpu.CompilerParams(
          dimension_semantics=("arbitrary",), vmem_limit_bytes=VMEM_LIMIT_V7X),
      name="ffn_final",
  )(x_ctx, x_lat, mod3, n2, wi, wo, fg)


def kernel(x_prompt, x_sample, state_hgrn, c, c_ctx, w_ada, b_ada, norm1_g, w_in, hgrn_lb_logits,
           hgrn_norm_g, w_branch_a, pool_w, pool_scale, w_branch_b, w_out, norm2_g, w_ffn_in,
           w_ffn_out, final_g):
  bp, sp, _ = x_prompt.shape
  bs, ss, _ = x_sample.shape
  depth = w_in.shape[0]
  assert bs + 1 <= 8 and ss % GRID_W == 0
  xp = x_prompt.reshape(bp * sp, D_MODEL)
  xs = x_sample.reshape(bs * ss, D_MODEL)
  cvec = jnp.concatenate([c_ctx[None, :], c, jnp.zeros((8 - 1 - bs, D_MODEL), F32)], axis=0)
  row2 = lambda a: a.reshape(1, -1)
  ffn_tt = 512
  ctx_states = []
  for l in range(depth):
    mod, win16, wa16, wb16, wo16, pw16 = _mod_call(
        cvec, w_ada[l], row2(b_ada[l]), [w_in[l], w_branch_a[l], w_branch_b[l], w_out[l]],
        [pool_w[l]])
    mod3 = mod[:1 + bs, None, :]
    mix_w = (hgrn_lb_logits, row2(norm1_g[l]), win16, row2(hgrn_norm_g[l]), wa16, pw16,
             row2(pool_scale[l]), wb16, wo16)
    assert l == depth - 1, "the FFN call also applies the final norm"
    xp, st = _mix_call(xp, mod3, lambda i: 0, *mix_w, None, layer=l, n_seq=2, seq_len=sp,
                       grid2d=False, emit_state=True)
    ctx_states.append(st)
    (xs,) = _mix_call(xs, mod3, lambda i: 1 + i, *mix_w, state_hgrn, layer=l, n_seq=1,
                      seq_len=ss, grid2d=True, emit_state=False)
    xp, xs = _ffn_call(xp, xs, mod3, row2(norm2_g[l]), w_ffn_in[l], w_ffn_out[l], row2(final_g),
                       tt=ffn_tt, lat_seq_len=ss)
  new_state = jnp.stack(ctx_states, axis=1).astype(x_prompt.dtype)
  return (xp.reshape(bp, sp, D_MODEL), xs.reshape(bs, ss, D_MODEL), new_state)
```

```python
import functools

import jax
import jax.numpy as jnp
from jax import lax
from jax.experimental import pallas as pl
from jax.experimental.pallas import tpu as pltpu

F32 = jnp.float32
BF16 = jnp.bfloat16

D_MODEL = 1024
D_A = 512
N_HEADS = 4
HEAD = D_A // N_HEADS
D_B = 512
N_GROUPS = 4
GROUP = D_B // N_GROUPS
POOL_WINDOWS = (2, 4, 8, 16)
GRID_W = 64
D_FF = 2816
EPS = 1e-6

CHUNK = 128
DIAG = 16
CHUNK_SLOTS = 2
PROJ_COLS = 256
N_SCAN_IN = 4
ROWS = 256
EXP_CLAMP = 80.0
VMEM_LIMIT_V7X = 58 * 1024 * 1024


def _sigmoid(x):
  return 0.5 + 0.5 * jnp.tanh(0.5 * x)


def _silu(x):
  h = 0.5 * x
  return h + h * jnp.tanh(h)


def _dot(a, b):
  return jnp.dot(a, b, preferred_element_type=F32)


def _dot_tn(a, b):
  return lax.dot_general(a, b, (((0,), (0,)), ((), ())), preferred_element_type=F32)


def _split_bf16(x):
  hi = x.astype(BF16)
  lo = (x - hi.astype(F32)).astype(BF16)
  return hi, lo


def _rms_rows(x):
  return x * lax.rsqrt(jnp.mean(x * x, axis=-1, keepdims=True) + EPS)


PREP_STEPS = 8


def _mod_body(c_ref, w_ref, b_ref, *refs):
  n_w = (len(refs) - 1) // 2
  o_ref = refs[n_w]
  cv = c_ref[...]
  s = _silu(cv).astype(BF16)
  o_ref[...] = _dot(s, w_ref[...].astype(BF16)) + b_ref[...]
  for src, dst in zip(refs[:n_w], refs[n_w + 1:]):
    dst[...] = src[...].astype(BF16)


def _mod_call(cvec, w_ada, b_ada, mats, stacks):
  n = w_ada.shape[1]
  assert n % (PREP_STEPS * 128) == 0
  bn = n // PREP_STEPS
  col = lambda j: (0, j)
  in_specs = [pl.BlockSpec((8, D_MODEL), lambda j: (0, 0)),
              pl.BlockSpec((D_MODEL, bn), col),
              pl.BlockSpec((1, bn), col)]
  w_specs, w_shapes = [], []
  for w in mats:
    rows, cols = w.shape
    assert cols % (PREP_STEPS * 128) == 0
    w_specs.append(pl.BlockSpec((rows, cols // PREP_STEPS), col))
    w_shapes.append(jax.ShapeDtypeStruct(w.shape, BF16))
  for w in stacks:
    g, k, m = w.shape
    assert PREP_STEPS % g == 0
    per = PREP_STEPS // g
    w_specs.append(pl.BlockSpec((None, k, m), lambda j, per=per: (j // per, 0, 0)))
    w_shapes.append(jax.ShapeDtypeStruct(w.shape, BF16))
  return pl.pallas_call(
      _mod_body,
      grid=(PREP_STEPS,),
      in_specs=in_specs + w_specs,
      out_specs=[pl.BlockSpec((8, bn), col)] + w_specs,
      out_shape=[jax.ShapeDtypeStruct((8, n), F32)] + w_shapes,
      compiler_params=pltpu.CompilerParams(
          dimension_semantics=("arbitrary",), vmem_limit_bytes=VMEM_LIMIT_V7X),
      name="adaln_mod",
  )(cvec, w_ada, b_ada, *mats, *stacks)


N_BLOCKS = CHUNK // DIAG
N_UP = N_BLOCKS.bit_length() - 1


def _aligned(row):
  return row if isinstance(row, int) else pl.multiple_of(row, CHUNK)


def _query_blocks(level, dirn):
  return [blk for blk in range(N_BLOCKS) if ((blk >> (level - 1)) & 1) != dirn]


def _level_row(level, block, dirn):
  half = DIAG << (level - 1)
  start = (block >> level) * 2 * half
  return start + (half - 1 if dirn == 0 else half)


def _mix_body(*refs, layer, n_seq, seq_len, grid2d, has_state, emit_state):
  (x_ref, mod_ref, lbl_ref, n1_ref, win_ref, ng_ref, wa_ref, pw_ref, ps_ref, wb_ref,
   wo_ref) = refs[:11]
  rest = list(refs[11:])
  s0_ref = rest.pop(0) if has_state else None
  out_ref = rest.pop(0)
  st_ref = rest.pop(0) if emit_state else None
  (h_s, q_s, zf_s, v_s, og_s, zp_s, ga_s, gb_s, b_c, k_c, o_s, pm_s, st_s, tri_s, lev_s,
   band_s, icnt_s, tab_s, xt_s) = rest[:19]
  cp_s = rest[19] if grid2d else None

  tt = n_seq * seq_len
  cps = seq_len // CHUNK

  ti = lax.broadcasted_iota(jnp.int32, (CHUNK, CHUNK), 0)
  si = lax.broadcasted_iota(jnp.int32, (CHUNK, CHUNK), 1)
  xr = ti ^ si
  lvl = jnp.zeros((CHUNK, CHUNK), jnp.int32)
  for j in range(1, N_UP + 1):
    lvl = jnp.where(xr >= (DIAG << (j - 1)), j, lvl)
  lev_s[0] = jnp.where(ti >= si, lvl, -1)
  lev_s[1] = jnp.where(ti <= si, lvl, -1)
  for dirn, tri in enumerate((ti >= si, ti <= si)):
    tri_s[dirn] = jnp.concatenate([tri.astype(BF16)] * 2, axis=1)

  seg_shift = (GRID_W if grid2d else ROWS).bit_length() - 1
  tr = lax.broadcasted_iota(jnp.int32, (ROWS, ROWS), 0)
  sr = lax.broadcasted_iota(jnp.int32, (ROWS, ROWS), 1)
  same_seg = (tr >> seg_shift) == (sr >> seg_shift)
  for g, w in enumerate(POOL_WINDOWS):
    dlt = sr - tr
    band = (same_seg & (dlt >= -(w // 2)) & (dlt <= w // 2 - 1)).astype(F32)
    band_s[g] = band.astype(BF16)
    icnt_s[g] = 1.0 / jnp.sum(band, axis=1, keepdims=True)

  mod = mod_ref[...]
  sh1 = mod[:, 0:D_MODEL]
  a1 = n1_ref[...] * (1.0 + mod[:, D_MODEL:2 * D_MODEL])
  gate1 = mod[:, 2 * D_MODEL:3 * D_MODEL]
  lg = lbl_ref[...]
  e = jnp.exp(lg - jnp.max(lg, axis=0, keepdims=True))
  lb = jnp.sum(e[:layer + 1], axis=0) / jnp.sum(e, axis=0)

  def run(units):
    for u in units:
      u()

  def interleave(main, side):
    out, taken = [], 0
    for k, u in enumerate(main):
      out.append(u)
      want = ((k + 1) * len(side)) // len(main)
      out.extend(side[taken:want])
      taken = want
    return out

  def phase1_units(r0, n):
    rr = slice(r0, r0 + n)

    def norm():
      for i in range(n // ROWS):
        r = slice(r0 + i * ROWS, r0 + (i + 1) * ROWS)
        h_s[r, :] = (_rms_rows(x_ref[r, :]) * a1 + sh1).astype(BF16)

    def proj(c0, c1):
      return _dot(h_s[rr, :], win_ref[:, c0:c1])

    def put(dst, cols, c0, act, dtype):
      def unit():
        z = proj(c0, c0 + (cols.stop - cols.start))
        dst[rr, cols] = (z if act is None else act(z)).astype(dtype)
      return unit

    layout = [(q_s, 0, D_A, None, F32), (zf_s, 0, D_A, None, F32), (zf_s, D_A, D_A, None, F32),
              (v_s, 0, D_A, None, BF16), (og_s, 0, D_A, _silu, BF16), (zp_s, 0, D_B, None, F32),
              (ga_s, 0, D_MODEL, _sigmoid, BF16), (gb_s, 0, D_MODEL, _sigmoid, BF16)]
    units, n_scan_in, c0 = [norm], None, 0
    for k, (dst, d0, width, act, dtype) in enumerate(layout):
      for j in range(0, width, PROJ_COLS):
        units.append(put(dst, slice(d0 + j, d0 + j + PROJ_COLS), c0 + j, act, dtype))
      c0 += width
      if k + 1 == N_SCAN_IN:
        n_scan_in = len(units)
    return units, n_scan_in

  half_c = 0.5 * (1.0 - lb)
  blk_i = lax.broadcasted_iota(jnp.int32, (N_BLOCKS, HEAD), 0)
  zero_tile = jnp.zeros((DIAG, HEAD), BF16)
  t_eb, t_et = 2 * N_UP, 2 * N_UP + 1

  blk_rows = lambda a, blk: a[blk * DIAG:(blk + 1) * DIAG]

  def stage_gates(base, dirn, slot):
    sd = 2 * slot + dirn
    c1 = half_c[dirn:dirn + 1, :]
    m = c1 * jnp.tanh(0.5 * zf_s[pl.ds(base, CHUNK), dirn * D_A:(dirn + 1) * D_A])
    ghi, glo = _split_bf16(jnp.log((1.0 - c1) + m))
    bsum = _dot(tri_s[dirn], jnp.concatenate([ghi, glo], axis=0))
    for hd in range(N_HEADS):
      b_c[sd, hd] = bsum[:, hd * HEAD:(hd + 1) * HEAD]
    k_c[sd] = c1 - m

  def stage_factors(base, dirn, zero_state, slot, hd):
    sd = 2 * slot + dirn
    cs = slice(hd * HEAD, (hd + 1) * HEAD)
    crow = DIAG // 2 if dirn == 0 else DIAG // 2 - 1
    tot_row = CHUNK - 1 if dirn == 0 else 0
    brow = lambda r: b_c[sd, hd, r:r + 1, :]
    tot = brow(tot_row)
    beta = b_c[sd, hd, pl.ds(crow, N_BLOCKS, stride=DIAG), :]
    for l in range(1, N_UP + 1):
      n_par = N_BLOCKS >> l
      ref = brow(_level_row(l, (n_par - 1) << l, dirn))
      for p in range(n_par - 2, -1, -1):
        ref = jnp.where(blk_i < ((p + 1) << l), brow(_level_row(l, p << l, dirn)), ref)
      d = beta - ref
      tab_s[sd, hd, l - 1] = jnp.exp(jnp.minimum(d, 0.0))
      tab_s[sd, hd, N_UP + l - 1] = jnp.exp(jnp.minimum(-d, 0.0))
    tab_s[sd, hd, t_eb] = jnp.exp(beta)
    tab_s[sd, hd, t_et] = jnp.exp(tot - beta)
    trow = lambda t, blk: jnp.broadcast_to(tab_s[sd, hd, t, blk:blk + 1, :], (DIAG, HEAD))

    beta_b = jnp.concatenate(
        [jnp.broadcast_to(brow(blk * DIAG + crow), (DIAG, HEAD)) for blk in range(N_BLOCKS)],
        axis=0)
    de = b_c[sd, hd] - beta_b
    a0 = q_s[pl.ds(base, CHUNK), cs] * jnp.exp(jnp.minimum(de, EXP_CLAMP))
    b0 = k_c[sd, :, cs] * jnp.exp(jnp.minimum(-de, EXP_CLAMP))
    ch = sd * N_HEADS + hd
    ops = dict(base=base, dirn=dirn, hd=hd, zero_state=zero_state, tot=tot, ch=ch,
               a0=a0.astype(BF16), lhs={})
    xt_s[ch, 0] = jnp.transpose(b0.astype(BF16))
    for l in range(1, N_UP + 1):
      qb = _query_blocks(l, dirn)
      ops["lhs"][l] = jnp.concatenate(
          [(blk_rows(a0, blk) * trow(l - 1, blk)).astype(BF16) for blk in qb], axis=0)
      xt_s[ch, l] = jnp.transpose(jnp.concatenate(
          [zero_tile if blk in qb else (blk_rows(b0, blk) * trow(N_UP + l - 1, blk)).astype(BF16)
           for blk in range(N_BLOCKS)], axis=0))
    ops["kt"] = jnp.concatenate(
        [(blk_rows(b0, blk) * trow(t_et, blk)).astype(BF16) for blk in range(N_BLOCKS)], axis=0)
    if not zero_state:
      ops["qt"] = jnp.concatenate(
          [(blk_rows(a0, blk) * trow(t_eb, blk)).astype(BF16) for blk in range(N_BLOCKS)], axis=0)
    return ops

  def stage_scores(ops):
    dirn = ops["dirn"]
    ch = ops["ch"]
    p_same = _dot(ops["a0"], xt_s[ch, 0])
    p_lev = {l: _dot(ops["lhs"][l], xt_s[ch, l]) for l in range(1, N_UP + 1)}
    sc_rows = []
    for blk in range(N_BLOCKS):
      lev = lev_s[dirn, blk * DIAG:(blk + 1) * DIAG, :]
      r = jnp.where(lev == 0, blk_rows(p_same, blk), 0.0)
      for l in range(1, N_UP + 1):
        qb = _query_blocks(l, dirn)
        if blk in qb:
          r = jnp.where(lev == l, blk_rows(p_lev[l], qb.index(blk)), r)
      sc_rows.append(r.astype(BF16))
    ops["sc"] = jnp.concatenate(sc_rows, axis=0)

  def stage_out(ops):
    dirn, hd = ops["dirn"], ops["hd"]
    rows = pl.ds(ops["base"], CHUNK)
    cs = slice(hd * HEAD, (hd + 1) * HEAD)
    vh = v_s[rows, cs]
    o = _dot(ops["sc"], vh)
    if ops["zero_state"]:
      st_s[dirn, hd] = _dot_tn(vh, ops["kt"])
    else:
      st = st_s[dirn, hd]
      xt_s[ops["ch"], N_UP + 1] = jnp.transpose(st).astype(BF16)
      o = o + _dot(ops["qt"], xt_s[ops["ch"], N_UP + 1])
      st_s[dirn, hd] = st * jnp.exp(ops["tot"]) + _dot_tn(vh, ops["kt"])
    o_s[dirn, rows, cs] = o

  def scan_units(chunks):
    group = []
    units = [functools.partial(stage_gates, base, dirn, slot) for base, dirn, _, slot in chunks]
    for base, dirn, zero_state, slot in chunks:
      for hd in range(N_HEADS):
        units.append(lambda a=(base, dirn, zero_state, slot, hd): group.append(stage_factors(*a)))
    n = len(chunks) * N_HEADS
    units += [lambda k=k: stage_scores(group[k]) for k in range(n)]
    units += [lambda k=k: stage_out(group[k]) for k in range(n)]
    return units

  def chunk_pair(seq_base, ci, zero_state, slot):
    return [(_aligned(seq_base + ci * CHUNK), 0, zero_state, slot),
            (_aligned(seq_base + (cps - 1 - ci) * CHUNK), 1, zero_state, slot)]

  def emit_states(sq):
    for dirn in range(2):
      for hd in range(N_HEADS):
        st_ref[sq, dirn, hd] = st_s[dirn, hd].T

  def band_pool(r, g):
    gs = slice(g * GROUP, (g + 1) * GROUP)
    return _dot(band_s[g], zp_s[r, gs].astype(BF16)) * icnt_s[g]

  def pool1d_units(r0):
    r = slice(r0, r0 + ROWS)

    def unit(g):
      gs = slice(g * GROUP, (g + 1) * GROUP)
      pm_s[r, gs] = (band_pool(r, g) - zp_s[r, gs]).astype(BF16)
    return [functools.partial(unit, g) for g in range(N_GROUPS)]

  def pool2d_units():
    n_rows = seq_len // GRID_W

    def cols(r0, g):
      r = slice(r0, r0 + ROWS)
      cp_s[r, g * GROUP:(g + 1) * GROUP] = band_pool(r, g)

    def rows(g, w):
      gs = slice(g * GROUP, (g + 1) * GROUP)
      acc = None
      lo_prev = hi_prev = None
      for rr in range(n_rows):
        lo = max(rr - w // 2, 0)
        hi = min(rr + w // 2 - 1, n_rows - 1)
        if rr == 0:
          acc = cp_s[0:GRID_W, gs]
          for k in range(1, hi + 1):
            acc = acc + cp_s[k * GRID_W:(k + 1) * GRID_W, gs]
        else:
          if hi > hi_prev:
            acc = acc + cp_s[hi * GRID_W:(hi + 1) * GRID_W, gs]
          if lo > lo_prev:
            acc = acc - cp_s[lo_prev * GRID_W:(lo_prev + 1) * GRID_W, gs]
        lo_prev, hi_prev = lo, hi
        rs = slice(rr * GRID_W, (rr + 1) * GRID_W)
        pm_s[rs, gs] = (acc * (1.0 / (hi - lo + 1)) - zp_s[rs, gs]).astype(BF16)

    return ([functools.partial(cols, r0, g) for r0 in range(0, tt, ROWS) for g in range(N_GROUPS)]
            + [functools.partial(rows, g, w) for g, w in enumerate(POOL_WINDOWS)])

  ng = ng_ref[...]
  psc = ps_ref[...]

  def phase4_units(r0):
    r = slice(r0, r0 + ROWS)
    vals = {}

    def branch_a():
      parts = []
      for hd in range(N_HEADS):
        cs = slice(hd * HEAD, (hd + 1) * HEAD)
        parts.append(_rms_rows(o_s[0, r, cs] + o_s[1, r, cs]) * ng[:, cs])
      oa = (jnp.concatenate(parts, axis=1) * og_s[r, :].astype(F32)).astype(BF16)
      vals["ya"] = _dot(oa, wa_ref[...])

    def branch_b():
      obp = []
      for g in range(N_GROUPS):
        gs = slice(g * GROUP, (g + 1) * GROUP)
        obp.append(_dot(pm_s[r, gs], pw_ref[g]) * psc[:, gs])
      ob = jnp.concatenate(obp, axis=1).astype(BF16)
      vals["yb"] = _dot(ob, wb_ref[...])

    def merge_out():
      merged = (ga_s[r, :].astype(F32) * vals["ya"]
                + gb_s[r, :].astype(F32) * vals["yb"]).astype(BF16)
      out_ref[r, :] = x_ref[r, :] + gate1 * _dot(merged, wo_ref[...])

    return [branch_a, branch_b, merge_out]

  row_blocks = lambda r0, n: range(r0, r0 + n, ROWS)

  if has_state:
    assert n_seq == 1 and grid2d and not emit_state
    p1, n_scan_in = phase1_units(0, tt)
    run(p1[:n_scan_in])
    for dirn in range(2):
      for hd in range(N_HEADS):
        st_s[dirn, hd] = s0_ref[dirn, hd].T
    main = [u for cj in range(cps // CHUNK_SLOTS)
            for u in scan_units([c for slot in range(CHUNK_SLOTS)
                                 for c in chunk_pair(0, cj * CHUNK_SLOTS + slot, False, slot)])]
    run(interleave(main, p1[n_scan_in:] + pool2d_units()))
    for r0 in row_blocks(0, tt):
      run(phase4_units(r0))
  else:
    assert cps <= CHUNK_SLOTS and not grid2d
    p1, n_scan_in = phase1_units(0, tt)
    blocks = lambda sq: row_blocks(sq * seq_len, seq_len)

    run(p1[:n_scan_in])
    for sq in range(n_seq):
      main = scan_units([c for ci in range(cps)
                         for c in chunk_pair(sq * seq_len, ci, ci == 0, ci)])
      if emit_state:
        main.append(functools.partial(emit_states, sq))
      side = []
      if sq == 0:
        side += p1[n_scan_in:]
        side += [u for s2 in range(n_seq) for r0 in blocks(s2) for u in pool1d_units(r0)]
      else:
        side += [u for r0 in blocks(sq - 1) for u in phase4_units(r0)]
      run(interleave(main, side))
    run([u for r0 in blocks(n_seq - 1) for u in phase4_units(r0)])


def _const_spec(shape):
  nd = len(shape)
  return pl.BlockSpec(shape, lambda i, _nd=nd: (0,) * _nd, pipeline_mode=pl.Buffered(1))


def _mix_call(x2d, mod3, mod_row, lbl, n1, win, ng, wa, pw, ps, wb, wo, s0, *, layer, n_seq,
              seq_len, grid2d, emit_state):
  tokens = x2d.shape[0]
  tt = n_seq * seq_len
  assert tokens % tt == 0 and seq_len % (CHUNK * CHUNK_SLOTS) == 0 and tt % ROWS == 0
  assert CHUNK == HEAD
  steps = tokens // tt
  has_state = s0 is not None
  x_mode = pl.Buffered(1) if steps <= 2 else None
  in_specs = [
      pl.BlockSpec((tt, D_MODEL), lambda i: (i, 0), pipeline_mode=x_mode),
      pl.BlockSpec((None, 1, 6 * D_MODEL), lambda i: (mod_row(i), 0, 0)),
      _const_spec(lbl.shape), _const_spec(n1.shape), _const_spec(win.shape), _const_spec(ng.shape),
      _const_spec(wa.shape), _const_spec(pw.shape), _const_spec(ps.shape), _const_spec(wb.shape),
      _const_spec(wo.shape),
  ]
  args = [x2d, mod3, lbl, n1, win, ng, wa, pw, ps, wb, wo]
  if has_state:
    in_specs.append(pl.BlockSpec((None, None, 2, N_HEADS, HEAD, HEAD),
                                 lambda i: (i, layer, 0, 0, 0, 0)))
    args.append(s0)
  out_specs = [pl.BlockSpec((tt, D_MODEL), lambda i: (i, 0))]
  out_shape = [jax.ShapeDtypeStruct((tokens, D_MODEL), F32)]
  if emit_state:
    out_specs.append(pl.BlockSpec((n_seq, 2, N_HEADS, HEAD, HEAD), lambda i: (i, 0, 0, 0, 0)))
    out_shape.append(jax.ShapeDtypeStruct((tokens // seq_len, 2, N_HEADS, HEAD, HEAD), F32))
  scratch = [
      pltpu.VMEM((tt, D_MODEL), BF16),
      pltpu.VMEM((tt, D_A), F32),
      pltpu.VMEM((tt, 2 * D_A), F32),
      pltpu.VMEM((tt, D_A), BF16),
      pltpu.VMEM((tt, D_A), BF16),
      pltpu.VMEM((tt, D_B), F32),
      pltpu.VMEM((tt, D_MODEL), BF16),
      pltpu.VMEM((tt, D_MODEL), BF16),
      pltpu.VMEM((2 * CHUNK_SLOTS, N_HEADS, CHUNK, HEAD), F32),
      pltpu.VMEM((2 * CHUNK_SLOTS, CHUNK, D_A), F32),
      pltpu.VMEM((2, tt, D_A), F32),
      pltpu.VMEM((tt, D_B), BF16),
      pltpu.VMEM((2, N_HEADS, HEAD, HEAD), F32),
      pltpu.VMEM((2, CHUNK, 2 * CHUNK), BF16),
      pltpu.VMEM((2, CHUNK, CHUNK), jnp.int32),
      pltpu.VMEM((N_GROUPS, ROWS, ROWS), BF16),
      pltpu.VMEM((N_GROUPS, ROWS, 1), F32),
      pltpu.VMEM((2 * CHUNK_SLOTS, N_HEADS, 2 * N_UP + 2, N_BLOCKS, HEAD), F32),
      pltpu.VMEM((2 * CHUNK_SLOTS * N_HEADS, N_UP + 2, HEAD, CHUNK), BF16),
  ]
  if grid2d:
    scratch.append(pltpu.VMEM((tt, D_B), F32))
  body = functools.partial(_mix_body, layer=layer, n_seq=n_seq, seq_len=seq_len, grid2d=grid2d,
                           has_state=has_state, emit_state=emit_state)
  return pl.pallas_call(
      body,
      grid=(steps,),
      in_specs=in_specs,
      out_specs=out_specs,
      out_shape=out_shape,
      scratch_shapes=scratch,
      compiler_params=pltpu.CompilerParams(
          dimension_semantics=("arbitrary",), vmem_limit_bytes=VMEM_LIMIT_V7X),
      name="mix_grid2d" if grid2d else "mix_seq1d",
  )(*args)


FFN_COLS = 256
FFN_ROWS = 256


WI_CHUNK = 512
WO_CHUNK = 256
STAGE_WINDOWS = 4


def _stream_cast(w_hbm, w_s, stage, sems, chunks):
  nw = stage.shape[0]

  def copy(k):
    return pltpu.make_async_copy(w_hbm.at[chunks[k]], stage.at[k % nw], sems.at[k % nw])

  for k in range(min(nw - 1, len(chunks))):
    copy(k).start()
  for k in range(len(chunks)):
    if k + nw - 1 < len(chunks):
      copy(k + nw - 1).start()
    copy(k).wait()
    w_s[chunks[k]] = stage[k % nw].astype(BF16)


def _ffn_body(xc_ref, xl_ref, mod_ref, n2_ref, wi_hbm, wo_hbm, fg_ref, oc_ref, ol_ref, h_s, hid_s,
              wi_s, wo_s, sti_s, sto_s, semi, semo, *, ctx_steps):
  tt = xc_ref.shape[0]
  step = pl.program_id(0)
  is_ctx = step < ctx_steps

  @pl.when(step == 0)
  def _():
    _stream_cast(wi_hbm, wi_s, sti_s, semi,
                 [(slice(None), pl.ds(c, WI_CHUNK)) for c in range(0, 2 * D_FF, WI_CHUNK)])
    _stream_cast(wo_hbm, wo_s, sto_s, semo,
                 [(pl.ds(r, WO_CHUNK), slice(None)) for r in range(0, D_FF, WO_CHUNK)])
    oc_ref[...] = jnp.zeros(oc_ref.shape, F32)

  mod = mod_ref[...]
  sh2 = mod[:, 3 * D_MODEL:4 * D_MODEL]
  a2 = n2_ref[...] * (1.0 + mod[:, 4 * D_MODEL:5 * D_MODEL])
  gate2 = mod[:, 5 * D_MODEL:6 * D_MODEL]
  fg = fg_ref[...]

  blocks = [slice(i * FFN_ROWS, (i + 1) * FFN_ROWS) for i in range(tt // FFN_ROWS)]
  pick_x = lambda r: jnp.where(is_ctx, xc_ref[r, :], xl_ref[r, :])
  for r in blocks:
    h_s[r, :] = (_rms_rows(pick_x(r)) * a2 + sh2).astype(BF16)
  for j in range(D_FF // FFN_COLS):
    cg = slice(j * FFN_COLS, (j + 1) * FFN_COLS)
    cu = slice(D_FF + j * FFN_COLS, D_FF + (j + 1) * FFN_COLS)
    for r in blocks:
      gp = _dot(h_s[r, :], wi_s[:, cg])
      up = _dot(h_s[r, :], wi_s[:, cu])
      hid_s[r, cg] = (_silu(gp) * up).astype(BF16)
  for r in blocks:
    y = _dot(hid_s[r, :], wo_s[...])
    new = _rms_rows(pick_x(r) + gate2 * y) * fg
    ol_ref[r, :] = new
    oc_ref[r, :] = jnp.where(is_ctx, new, oc_ref[r, :])


def _ffn_call(x_ctx, x_lat, mod3, n2, wi, wo, fg, *, tt, lat_seq_len):
  ctx_tokens, lat_tokens = x_ctx.shape[0], x_lat.shape[0]
  assert ctx_tokens % tt == 0 and lat_tokens % tt == 0 and lat_seq_len % tt == 0 and tt % ROWS == 0
  assert D_FF % FFN_COLS == 0 and (2 * D_FF) % WI_CHUNK == 0 and D_FF % WO_CHUNK == 0
  nc = ctx_tokens // tt
  steps_per_seq = lat_seq_len // tt
  mod_row = lambda i: jnp.where(i < nc, 0, 1 + (i - nc) // steps_per_seq)
  ctx_blk = lambda i: (jnp.minimum(i, nc - 1), 0)
  lat_blk = lambda i: (jnp.maximum(i - nc, 0), 0)
  body = functools.partial(_ffn_body, ctx_steps=nc)
  return pl.pallas_call(
      body,
      grid=((ctx_tokens + lat_tokens) // tt,),
      in_specs=[
          pl.BlockSpec((tt, D_MODEL), ctx_blk),
          pl.BlockSpec((tt, D_MODEL), lat_blk),
          pl.BlockSpec((None, 1, 6 * D_MODEL), lambda i: (mod_row(i), 0, 0)),
          _const_spec(n2.shape),
          pl.BlockSpec(memory_space=pl.ANY),
          pl.BlockSpec(memory_space=pl.ANY),
          _const_spec(fg.shape),
      ],
      out_specs=[
          pl.BlockSpec((tt, D_MODEL), ctx_blk),
          pl.BlockSpec((tt, D_MODEL), lat_blk),
      ],
      out_shape=[jax.ShapeDtypeStruct((ctx_tokens, D_MODEL), F32),
                 jax.ShapeDtypeStruct((lat_tokens, D_MODEL), F32)],
      scratch_shapes=[
          pltpu.VMEM((tt, D_MODEL), BF16),
          pltpu.VMEM((tt, D_FF), BF16),
          pltpu.VMEM((D_MODEL, 2 * D_FF), BF16),
          pltpu.VMEM((D_FF, D_MODEL), BF16),
          pltpu.VMEM((STAGE_WINDOWS, D_MODEL, WI_CHUNK), F32),
          pltpu.VMEM((STAGE_WINDOWS, WO_CHUNK, D_MODEL), F32),
          pltpu.SemaphoreType.DMA((STAGE_WINDOWS,)),
          pltpu.SemaphoreType.DMA((STAGE_WINDOWS,)),
      ],
      compiler_params=pltpu.CompilerParams(
          dimension_semantics=("arbitrary",), vmem_limit_bytes=VMEM_LIMIT_V7X),
      name="ffn_final",
  )(x_ctx, x_lat, mod3, n2, wi, wo, fg)


def kernel(x_prompt, x_sample, state_hgrn, c, c_ctx, w_ada, b_ada, norm1_g, w_in, hgrn_lb_logits,
           hgrn_norm_g, w_branch_a, pool_w, pool_scale, w_branch_b, w_out, norm2_g, w_ffn_in,
           w_ffn_out, final_g):
  bp, sp, _ = x_prompt.shape
  bs, ss, _ = x_sample.shape
  depth = w_in.shape[0]
  assert bs + 1 <= 8 and ss % GRID_W == 0
  xp = x_prompt.reshape(bp * sp, D_MODEL)
  xs = x_sample.reshape(bs * ss, D_MODEL)
  cvec = jnp.concatenate([c_ctx[None, :], c, jnp.zeros((8 - 1 - bs, D_MODEL), F32)], axis=0)
  row2 = lambda a: a.reshape(1, -1)
  ffn_tt = 512
  ctx_states = []
  for l in range(depth):
    mod, win16, wa16, wb16, wo16, pw16 = _mod_call(
        cvec, w_ada[l], row2(b_ada[l]), [w_in[l], w_branch_a[l], w_branch_b[l], w_out[l]],
        [pool_w[l]])
    mod3 = mod[:1 + bs, None, :]
    mix_w = (hgrn_lb_logits, row2(norm1_g[l]), win16, row2(hgrn_norm_g[l]), wa16, pw16,
             row2(pool_scale[l]), wb16, wo16)
    assert l == depth - 1, "the FFN call also applies the final norm"
    xp, st = _mix_call(xp, mod3, lambda i: 0, *mix_w, None, layer=l, n_seq=2, seq_len=sp,
                       grid2d=False, emit_state=True)
    ctx_states.append(st)
    (xs,) = _mix_call(xs, mod3, lambda i: 1 + i, *mix_w, state_hgrn, layer=l, n_seq=1,
                      seq_len=ss, grid2d=True, emit_state=False)
    xp, xs = _ffn_call(xp, xs, mod3, row2(norm2_g[l]), w_ffn_in[l], w_ffn_out[l], row2(final_g),
                       tt=ffn_tt, lat_seq_len=ss)
  new_state = jnp.stack(ctx_states, axis=1).astype(x_prompt.dtype)
  return (xp.reshape(bp, sp, D_MODEL), xs.reshape(bs, ss, D_MODEL), new_state)
```
